```python
import jax, jax.numpy as jnp
from jax import lax
import numpy as np

D_MODEL = 1024
BATCH = 8
SEQ = 4096
DEPTH = 2

HG_EXPAND = 128
HG_HEADS = D_MODEL // HG_EXPAND
HG_DK = HG_EXPAND
HG_DV = D_MODEL // HG_HEADS
HG_QK = HG_HEADS * HG_DK
HG_WIDTH = HG_HEADS * HG_DV
HG_CHUNK = 64
RET_HEADS = D_MODEL // 256
RET_DK = D_MODEL // RET_HEADS
RET_DV = 2 * RET_DK
RET_QK = RET_HEADS * RET_DK
RET_WIDTH = RET_HEADS * RET_DV
RET_CHUNK = 128
ROPE_BASE = 10000.0
D_FF = 4 * D_MODEL
DEEPNORM_ALPHA = (2 * DEPTH) ** 0.25
DEEPNORM_BETA = (8 * DEPTH) ** -0.25
LN_EPS = 1e-5
HEAD_NORM_EPS = 1e-6
ADA_SCALE = 0.1
IN_SIZES = (HG_QK, HG_QK, HG_WIDTH, HG_WIDTH,
            RET_QK, RET_QK, RET_WIDTH, RET_WIDTH,
            D_MODEL, D_MODEL)
IN_WIDTH = sum(IN_SIZES)

kernel_name = "hgrn2_retention_gated_hybrid_deepnorm_adaln"


def _layer_norm(x, g, b):
    x32 = x.astype(jnp.float32)
    mu = jnp.mean(x32, axis=-1, keepdims=True)
    var = jnp.mean(jnp.square(x32 - mu), axis=-1, keepdims=True)
    return ((x32 - mu) * lax.rsqrt(var + LN_EPS)).astype(x.dtype) * g + b


def _head_norm(o, centered):
    if centered:
        o = o - jnp.mean(o, axis=-1, keepdims=True)
    return o * lax.rsqrt(jnp.mean(jnp.square(o), axis=-1, keepdims=True) + HEAD_NORM_EPS)


def _to_chunks(t, chunk):
    b, s, h, d = t.shape
    return t.reshape(b, s // chunk, chunk, h, d).transpose(1, 0, 3, 2, 4)


def _from_chunks(t):
    n, b, h, c, d = t.shape
    return t.transpose(1, 0, 3, 2, 4).reshape(b, n * c, h, d)


def _hgrn2(q, zf, v, g, lb):
    B, S, _ = q.shape
    f32 = jnp.float32
    zf = zf.astype(f32)
    lb = lb.astype(f32)
    q = jax.nn.silu(q.astype(f32)).reshape(B, S, HG_HEADS, HG_DK)
    log_f = jnp.logaddexp(jnp.log(lb), jnp.log1p(-lb) + jax.nn.log_sigmoid(zf))
    k = (1.0 - lb) * jax.nn.sigmoid(-zf)
    log_f = log_f.reshape(B, S, HG_HEADS, HG_DK)
    k = k.reshape(B, S, HG_HEADS, HG_DK)
    v = v.astype(f32).reshape(B, S, HG_HEADS, HG_DV)
    causal = jnp.tril(jnp.ones((HG_CHUNK, HG_CHUNK), dtype=bool))[:, :, None]

    def step(state, blk):
        qc, lfc, kc, vc = blk
        b = jnp.cumsum(lfc, axis=2)
        rel = jnp.where(causal, b[:, :, :, None, :] - b[:, :, None, :, :], -jnp.inf)
        scores = jnp.einsum('bhid,bhijd,bhjd->bhij', qc, jnp.exp(rel), kc)
        b_last = b[:, :, -1:, :]
        o = (jnp.einsum('bhij,bhje->bhie', scores, vc)
             + jnp.einsum('bhid,bhde->bhie', qc * jnp.exp(b), state))
        state = (jnp.exp(b_last[:, :, 0, :, None]) * state
                 + jnp.einsum('bhjd,bhje->bhde', kc * jnp.exp(b_last - b), vc))
        return state, o

    state0 = jnp.zeros((B, HG_HEADS, HG_DK, HG_DV), f32)
    blocks = (_to_chunks(q, HG_CHUNK), _to_chunks(log_f, HG_CHUNK),
              _to_chunks(k, HG_CHUNK), _to_chunks(v, HG_CHUNK))
    _, o = lax.scan(step, state0, blocks)
    o = _head_norm(_from_chunks(o), centered=False).reshape(B, S, HG_WIDTH)
    return (o * jax.nn.silu(g.astype(f32))).astype(g.dtype)


def _rotate(t, cos, sin):
    B, S, H, d = t.shape
    t = t.reshape(B, S, H, d // 2, 2)
    t1, t2 = t[..., 0], t[..., 1]
    return jnp.stack([t1 * cos - t2 * sin, t1 * sin + t2 * cos], axis=-1).reshape(B, S, H, d)


def _retention(q, k, v, g, positions):
    B, S, _ = q.shape
    f32 = jnp.float32
    theta = ROPE_BASE ** (-jnp.linspace(0.0, 1.0, RET_DK // 2, dtype=f32))
    ang = positions.astype(f32)[:, :, None, None] * theta
    cos, sin = jnp.cos(ang), jnp.sin(ang)
    q = _rotate(q.astype(f32).reshape(B, S, RET_HEADS, RET_DK), cos, sin)
    k = _rotate(k.astype(f32).reshape(B, S, RET_HEADS, RET_DK), cos, sin) * (RET_DK ** -0.5)
    v = v.astype(f32).reshape(B, S, RET_HEADS, RET_DV)
    log_gamma = jnp.log1p(-jnp.exp2(-5.0 - jnp.arange(RET_HEADS, dtype=f32)))
    pos = jnp.arange(RET_CHUNK, dtype=f32)
    diff = pos[:, None] - pos[None, :]
    decay_mask = jnp.where(diff >= 0,
                           jnp.exp(log_gamma[:, None, None] * jnp.maximum(diff, 0.0)), 0.0)
    q_decay = jnp.exp(log_gamma[:, None] * (pos + 1.0))[:, :, None]
    k_decay = jnp.exp(log_gamma[:, None] * (RET_CHUNK - 1.0 - pos))[:, :, None]
    chunk_decay = jnp.exp(log_gamma * RET_CHUNK)[:, None, None]

    def step(state, blk):
        qc, kc, vc = blk
        scores = jnp.einsum('bhid,bhjd->bhij', qc, kc) * decay_mask
        o = (jnp.einsum('bhij,bhje->bhie', scores, vc)
             + jnp.einsum('bhid,bhde->bhie', qc * q_decay, state))
        state = chunk_decay * state + jnp.einsum('bhjd,bhje->bhde', kc * k_decay, vc)
        return state, o

    state0 = jnp.zeros((B, RET_HEADS, RET_DK, RET_DV), f32)
    blocks = (_to_chunks(q, RET_CHUNK), _to_chunks(k, RET_CHUNK), _to_chunks(v, RET_CHUNK))
    _, o = lax.scan(step, state0, blocks)
    o = _head_norm(_from_chunks(o), centered=True).reshape(B, S, RET_WIDTH)
    return (o * jax.nn.silu(g.astype(f32))).astype(g.dtype)


def setup_inputs(seed: int = 0) -> dict:
    key = jax.random.key(seed)
    ks = jax.random.split(key, 20)
    f32 = jnp.float32

    def nrm(k, shape, scale):
        return jax.random.normal(k, shape, f32) * scale

    x = nrm(ks[0], (BATCH, SEQ, D_MODEL), 1.0)
    c = nrm(ks[1], (BATCH, D_MODEL), 1.0)
    positions = jnp.broadcast_to(jnp.arange(SEQ, dtype=jnp.int32), (BATCH, SEQ))
    lb_logits = nrm(ks[2], (DEPTH, HG_QK), 0.5)
    w_ada = nrm(ks[3], (DEPTH, D_MODEL, 6 * D_MODEL), ADA_SCALE * D_MODEL ** -0.5)
    b_ada = nrm(ks[4], (DEPTH, 6 * D_MODEL), 0.01)
    w_in = nrm(ks[5], (DEPTH, D_MODEL, IN_WIDTH), D_MODEL ** -0.5)
    w_pa = nrm(ks[6], (DEPTH, HG_WIDTH, D_MODEL), HG_WIDTH ** -0.5)
    w_pb = nrm(ks[7], (DEPTH, RET_WIDTH, D_MODEL), RET_WIDTH ** -0.5)
    w_o = nrm(ks[8], (DEPTH, D_MODEL, D_MODEL), DEEPNORM_BETA * D_MODEL ** -0.5)
    ln1_g = 1.0 + nrm(ks[9], (DEPTH, D_MODEL), 0.02)
    ln1_b = nrm(ks[10], (DEPTH, D_MODEL), 0.02)
    w_up = nrm(ks[11], (DEPTH, D_MODEL, D_FF), D_MODEL ** -0.5)
    b_up = nrm(ks[12], (DEPTH, D_FF), 0.02)
    w_down = nrm(ks[13], (DEPTH, D_FF, D_MODEL), DEEPNORM_BETA * D_FF ** -0.5)
    b_down = nrm(ks[14], (DEPTH, D_MODEL), 0.02)
    ln2_g = 1.0 + nrm(ks[15], (DEPTH, D_MODEL), 0.02)
    ln2_b = nrm(ks[16], (DEPTH, D_MODEL), 0.02)
    return {"x": x, "c": c, "positions": positions, "lb_logits": lb_logits,
            "w_ada": w_ada, "b_ada": b_ada, "w_in": w_in, "w_pa": w_pa, "w_pb": w_pb,
            "w_o": w_o, "ln1_g": ln1_g, "ln1_b": ln1_b, "w_up": w_up, "b_up": b_up,
            "w_down": w_down, "b_down": b_down, "ln2_g": ln2_g, "ln2_b": ln2_b}


def reference(x, c, positions, lb_logits, w_ada, b_ada, w_in, w_pa, w_pb, w_o,
              ln1_g, ln1_b, w_up, b_up, w_down, b_down, ln2_g, ln2_b):
    lb_cum = jnp.cumsum(jax.nn.softmax(lb_logits.astype(jnp.float32), axis=0), axis=0)
    lower_bounds = lb_cum - lb_cum[:1]
    split_at = np.cumsum(IN_SIZES)[:-1].tolist()
    cond = jax.nn.silu(c)
    for l in range(DEPTH):
        mod = cond @ w_ada[l] + b_ada[l]
        shift1, scale1, gate1, shift2, scale2, gate2 = jnp.split(mod[:, None, :], 6, axis=-1)
        u = x * (1.0 + scale1) + shift1
        hq, hf, hi, hg, rq, rk, rv, rg, ga, gb = jnp.split(u @ w_in[l], split_at, axis=-1)
        ya = _hgrn2(hq, hf, hi, hg, lower_bounds[l]) @ w_pa[l]
        yb = _retention(rq, rk, rv, rg, positions) @ w_pb[l]
        y = (jax.nn.sigmoid(ga) * ya + jax.nn.sigmoid(gb) * yb) @ w_o[l]
        x = _layer_norm(DEEPNORM_ALPHA * x + (1.0 + gate1) * y, ln1_g[l], ln1_b[l])
        u = x * (1.0 + scale2) + shift2
        h = jnp.square(jax.nn.relu(u @ w_up[l] + b_up[l]))
        y = h @ w_down[l] + b_down[l]
        x = _layer_norm(DEEPNORM_ALPHA * x + (1.0 + gate2) * y, ln2_g[l], ln2_b[l])
    return x
```

```python
import functools
import math

import jax
import jax.numpy as jnp
from jax import lax
from jax.experimental import pallas as pl
from jax.experimental.pallas import tpu as pltpu

F32 = jnp.float32
BF16 = jnp.bfloat16

D_MODEL = 1024
HG_HEADS = 8
HG_D = 128
HG_PAIRS = HG_HEADS // 2
RET_HEADS = 4
RET_DK = 256
RET_DV = 512
RET_W = 2 * RET_DK + 2 * RET_DV
D_FF = 4 * D_MODEL
ROPE_BASE = 10000.0
LN_EPS = 1e-5
HEAD_NORM_EPS = 1e-6

MIX_TM = 256
HG_CHUNK = 128
MLP_TM = 512
VMEM_LIMIT = 60 * 1024 * 1024

NT_DIMS = (((1,), (1,)), ((), ()))
TN_DIMS = (((0,), (0,)), ((), ()))


def _resident(shape):
    n = len(shape)
    return pl.BlockSpec(shape, lambda *_: (0,) * n, pipeline_mode=pl.Buffered(1))


def _layer_norm(r, g, b):
    mu = jnp.mean(r, axis=-1, keepdims=True)
    d = r - mu
    var = jnp.mean(d * d, axis=-1, keepdims=True)
    return d * lax.rsqrt(var + LN_EPS) * g + b


def _silu(z):
    return z * jax.nn.sigmoid(z)


def _mod_kernel(c_ref, w_ref, b_ref, o_ref):
    cond = _silu(c_ref[...])
    o_ref[0] = jnp.dot(cond, w_ref[0], preferred_element_type=F32,
                       precision=lax.Precision.HIGHEST) + b_ref[0]


def _modulation(c, w_ada, b_ada):
    depth, d, n = w_ada.shape
    batch = c.shape[0]
    nb = n // d
    return pl.pallas_call(
        _mod_kernel,
        grid=(depth, nb),
        in_specs=[pl.BlockSpec((batch, d), lambda l, j: (0, 0)),
                  pl.BlockSpec((1, d, d), lambda l, j: (l, 0, j)),
                  pl.BlockSpec((1, 1, d), lambda l, j: (l, 0, j))],
        out_specs=pl.BlockSpec((1, batch, d), lambda l, j: (l, 0, j)),
        out_shape=jax.ShapeDtypeStruct((depth, batch, n), F32),
        name="ada_modulation",
    )(c, w_ada, b_ada.reshape(depth, 1, n))


def _lb_kernel(depth, logit_ref, o_ref):
    z = logit_ref[...]
    e = jnp.exp(z - jnp.max(z, axis=0, keepdims=True))
    sm = e / jnp.sum(e, axis=0, keepdims=True)
    o_ref[...] = jnp.zeros_like(o_ref)
    cum = jnp.zeros_like(sm[0:1])
    for l in range(depth):
        cum = cum + sm[l:l + 1]
        lb = cum - sm[0:1]
        o_ref[l, 0:1, :] = jnp.log(lb)
        o_ref[l, 1:2, :] = jnp.log1p(-lb)
        o_ref[l, 2:3, :] = 1.0 - lb


def _lower_bound_tables(lb_logits):
    depth, n = lb_logits.shape
    return pl.pallas_call(
        functools.partial(_lb_kernel, depth),
        out_shape=jax.ShapeDtypeStruct((depth, 8, n), F32),
        name="hgrn_lower_bounds",
    )(lb_logits.astype(F32))


def _level_reference(b, b_ref, level):
    c = b.shape[0]
    half = 1 << level
    if half >= 8:
        pieces = []
        for p in range(c // (2 * half)):
            mid = p * 2 * half + half - 1
            pieces.append(jnp.broadcast_to(b_ref[mid:mid + 1, :], (2 * half, HG_D)))
        return pieces[0] if len(pieces) == 1 else jnp.concatenate(pieces, axis=0)
    b3 = b.reshape(c // 8, 8, HG_D)
    sub = lax.broadcasted_iota(jnp.int32, b3.shape, 1)
    rho = None
    for p in reversed(range(8 // (2 * half))):
        mid = p * 2 * half + half - 1
        piece = jnp.broadcast_to(b3[:, mid:mid + 1, :], b3.shape)
        rho = piece if rho is None else jnp.where(sub < (p + 1) * 2 * half, piece, rho)
    return rho.reshape(c, HG_D)


def _hgrn_chunk(q, lf, k, v, st_ref, head, b_ref):
    c = q.shape[0]
    row = lax.broadcasted_iota(jnp.int32, (c, HG_D), 0)
    b = lf
    shift = 1
    while shift < c:
        b = b + jnp.where(row >= shift, pltpu.roll(b, shift, 0), 0.0)
        shift *= 2
    b_ref[...] = b
    ri = lax.broadcasted_iota(jnp.int32, (c, c), 0)
    ci = lax.broadcasted_iota(jnp.int32, (c, c), 1)
    differ = ri ^ ci
    lower = ri > ci
    scores = jnp.zeros((c, c), F32)
    for level in range(c.bit_length() - 1):
        rho = _level_reference(b, b_ref, level)
        in_right = ((row >> level) & 1) == 1
        z = (jnp.where(in_right, q, k) * jnp.exp(-jnp.abs(b - rho))).astype(BF16)
        s_level = lax.dot_general(z, z, NT_DIMS, preferred_element_type=F32)
        scores = jnp.where(((differ >> level) == 1) & lower, s_level, scores)
    diag = jnp.sum(q * k, axis=-1, keepdims=True)
    v16 = v.astype(BF16)
    o = jnp.dot(scores.astype(BF16), v16, preferred_element_type=F32) + diag * v
    st = st_ref[head]
    o = o + lax.dot_general((q * jnp.exp(b)).astype(BF16), st.astype(BF16), NT_DIMS,
                            preferred_element_type=F32)
    b_last = b_ref[c - 1:c, :]
    k_dec = (k * jnp.exp(b_last - b)).astype(BF16)
    st_ref[head] = st * jnp.exp(b_last) + lax.dot_general(v16, k_dec, TN_DIMS,
                                                          preferred_element_type=F32)
    return o


def _hgrn_pair(p, u_ref, wh_ref, lbt_ref, proj_ref, oa_ref, st_ref, b_ref):
    tm = u_ref.shape[0]
    proj_ref[:, 0:4 * 2 * HG_D] = jnp.dot(u_ref[...], wh_ref[p], preferred_element_type=F32)
    for j in range(2):
        head = 2 * p + j
        tab = lbt_ref[head]
        log_lb, log1m_lb, one_m_lb = tab[0:1], tab[1:2], tab[2:3]
        for ck in range(tm // HG_CHUNK):
            rows = slice(ck * HG_CHUNK, (ck + 1) * HG_CHUNK)
            zq = proj_ref[rows, j * HG_D:(j + 1) * HG_D]
            zf = proj_ref[rows, (2 + j) * HG_D:(3 + j) * HG_D]
            v = proj_ref[rows, (4 + j) * HG_D:(5 + j) * HG_D]
            zg = proj_ref[rows, (6 + j) * HG_D:(7 + j) * HG_D]
            e = jnp.exp(-jnp.abs(zf))
            log_sig = jnp.minimum(zf, 0.0) - jnp.log1p(e)
            a2 = log1m_lb + log_sig
            lf = jnp.maximum(log_lb, a2) + jnp.log1p(jnp.exp(-jnp.abs(log_lb - a2)))
            k = one_m_lb * (jnp.where(zf >= 0.0, e, 1.0) / (1.0 + e))
            o = _hgrn_chunk(_silu(zq), lf, k, v, st_ref, head, b_ref)
            o = o * lax.rsqrt(jnp.mean(o * o, axis=-1, keepdims=True) + HEAD_NORM_EPS)
            oa_ref[p, rows, j * HG_D:(j + 1) * HG_D] = (o * _silu(zg)).astype(BF16)


def _retention_head(h, u_ref, wr_ref, rtab_ref, cs_ref, proj_ref, ob_ref, st_ref):
    c = u_ref.shape[0]
    proj_ref[...] = jnp.dot(u_ref[...], wr_ref[h], preferred_element_type=F32)
    cos, sin = cs_ref[0], cs_ref[1]
    half = RET_DK // 2

    def rotate(off):
        t1 = proj_ref[:, off:off + half]
        t2 = proj_ref[:, off + half:off + RET_DK]
        return jnp.concatenate([t1 * cos - t2 * sin, t1 * sin + t2 * cos], axis=1)

    q = rotate(0)
    k = rotate(RET_DK) * (RET_DK ** -0.5)
    v16 = proj_ref[:, 2 * RET_DK:2 * RET_DK + RET_DV].astype(BF16)
    log_gamma_wide = rtab_ref[h][0:1, :]
    log_gamma = log_gamma_wide[:, 0:RET_DK]
    ri = lax.broadcasted_iota(jnp.int32, (c, c), 0)
    ci = lax.broadcasted_iota(jnp.int32, (c, c), 1)
    diff = (ri - ci).astype(F32)
    decay = jnp.where(diff >= 0.0, jnp.exp(log_gamma * jnp.maximum(diff, 0.0)), 0.0)
    pos = lax.broadcasted_iota(jnp.int32, (c, RET_DK), 0).astype(F32)
    q_decay = jnp.exp(log_gamma * (pos + 1.0))
    k_decay = jnp.exp(log_gamma * (c - 1.0 - pos))
    scores = lax.dot_general(q.astype(BF16), k.astype(BF16), NT_DIMS,
                             preferred_element_type=F32) * decay
    st = st_ref[h]
    o = (jnp.dot(scores.astype(BF16), v16, preferred_element_type=F32)
         + jnp.dot((q * q_decay).astype(BF16), st.astype(BF16), preferred_element_type=F32))
    chunk_decay = jnp.exp(log_gamma_wide * float(c))
    st_ref[h] = chunk_decay * st + lax.dot_general((k * k_decay).astype(BF16), v16, TN_DIMS,
                                                   preferred_element_type=F32)
    o = o - jnp.mean(o, axis=-1, keepdims=True)
    o = o * lax.rsqrt(jnp.mean(o * o, axis=-1, keepdims=True) + HEAD_NORM_EPS)
    g = proj_ref[:, 2 * RET_DK + RET_DV:RET_W]
    ob_ref[h] = (o * _silu(g)).astype(BF16)


def _mixer_kernel(alpha, x_ref, mod_ref, pos_ref, theta_ref, lbt_ref, rtab_ref,
                  wh_ref, wr_ref, wg_ref, wpa_ref, wpb_ref, wo_ref, lng_ref, lnb_ref,
                  o_ref, u_ref, proj_ref, oa_ref, ob_ref, cs_ref, sth_ref, str_ref, b_ref):
    @pl.when(pl.program_id(1) == 0)
    def _():
        sth_ref[...] = jnp.zeros_like(sth_ref)
        str_ref[...] = jnp.zeros_like(str_ref)

    x = x_ref[0]
    shift, scale, gate = mod_ref[0, 0:1, :], mod_ref[0, 1:2, :], mod_ref[0, 2:3, :]
    u_ref[...] = (x * (1.0 + scale) + shift).astype(BF16)
    ang = pos_ref[0].astype(F32) * theta_ref[...]
    cs_ref[0] = jnp.cos(ang)
    cs_ref[1] = jnp.sin(ang)

    def hgrn_body(p, carry):
        _hgrn_pair(p, u_ref, wh_ref, lbt_ref, proj_ref, oa_ref, sth_ref, b_ref)
        return carry

    lax.fori_loop(0, HG_PAIRS, hgrn_body, 0)

    def ret_body(h, carry):
        _retention_head(h, u_ref, wr_ref, rtab_ref, cs_ref, proj_ref, ob_ref, str_ref)
        return carry

    lax.fori_loop(0, RET_HEADS, ret_body, 0)

    ya = jnp.dot(oa_ref[0], wpa_ref[0], preferred_element_type=F32)
    for p in range(1, HG_PAIRS):
        ya = ya + jnp.dot(oa_ref[p], wpa_ref[p], preferred_element_type=F32)
    yb = jnp.dot(ob_ref[0], wpb_ref[0], preferred_element_type=F32)
    for h in range(1, RET_HEADS):
        yb = yb + jnp.dot(ob_ref[h], wpb_ref[h], preferred_element_type=F32)
    gates = jnp.dot(u_ref[...], wg_ref[...], preferred_element_type=F32)
    y = (jax.nn.sigmoid(gates[:, 0:D_MODEL]) * ya
         + jax.nn.sigmoid(gates[:, D_MODEL:2 * D_MODEL]) * yb)
    y = jnp.dot(y.astype(BF16), wo_ref[...], preferred_element_type=F32)
    o_ref[0] = _layer_norm(alpha * x + (1.0 + gate) * y, lng_ref[...], lnb_ref[...])


def _mixer(alpha, x, mod, pos, theta, lbt, rtab, wh, wr, wg, wpa, wpb, wo, ln_g, ln_b):
    batch, seq, d = x.shape
    tm = min(MIX_TM, seq)
    tile = pl.BlockSpec((1, tm, d), lambda b, s: (b, s, 0))
    return pl.pallas_call(
        functools.partial(_mixer_kernel, alpha),
        grid=(batch, seq // tm),
        in_specs=[tile,
                  pl.BlockSpec((1, 6, d), lambda b, s: (b, 0, 0)),
                  pl.BlockSpec((1, tm, 1), lambda b, s: (b, s, 0)),
                  _resident(theta.shape), _resident(lbt.shape), _resident(rtab.shape),
                  _resident(wh.shape), _resident(wr.shape), _resident(wg.shape),
                  _resident(wpa.shape), _resident(wpb.shape), _resident(wo.shape),
                  _resident(ln_g.shape), _resident(ln_b.shape)],
        out_specs=tile,
        out_shape=jax.ShapeDtypeStruct(x.shape, F32),
        scratch_shapes=[pltpu.VMEM((tm, d), BF16),
                        pltpu.VMEM((tm, RET_W), F32),
                        pltpu.VMEM((HG_PAIRS, tm, 2 * HG_D), BF16),
                        pltpu.VMEM((RET_HEADS, tm, RET_DV), BF16),
                        pltpu.VMEM((2, tm, RET_DK // 2), F32),
                        pltpu.VMEM((HG_HEADS, HG_D, HG_D), F32),
                        pltpu.VMEM((RET_HEADS, RET_DK, RET_DV), F32),
                        pltpu.VMEM((HG_CHUNK, HG_D), F32)],
        compiler_params=pltpu.CompilerParams(
            dimension_semantics=("arbitrary", "arbitrary"), vmem_limit_bytes=VMEM_LIMIT),
        name="mixer_sublayer",
    )(x, mod, pos, theta, lbt, rtab, wh, wr, wg, wpa, wpb, wo, ln_g, ln_b)


def _mlp_kernel(alpha, x_ref, mod_ref, wup_ref, bup_ref, wdn_ref, bdn_ref, lng_ref, lnb_ref,
                o_ref, h_ref):
    x = x_ref[0]
    shift, scale, gate = mod_ref[0, 3:4, :], mod_ref[0, 4:5, :], mod_ref[0, 5:6, :]
    u = (x * (1.0 + scale) + shift).astype(BF16)
    for j in range(D_FF // D_MODEL):
        cols = slice(j * D_MODEL, (j + 1) * D_MODEL)
        hj = jnp.dot(u, wup_ref[:, cols], preferred_element_type=F32) + bup_ref[:, cols]
        h_ref[:, cols] = jnp.square(jnp.maximum(hj, 0.0)).astype(BF16)
    y = jnp.dot(h_ref[...], wdn_ref[...], preferred_element_type=F32) + bdn_ref[...]
    o_ref[0] = _layer_norm(alpha * x + (1.0 + gate) * y, lng_ref[...], lnb_ref[...])


def _mlp(alpha, x, mod, wup, bup, wdn, bdn, ln_g, ln_b):
    batch, seq, d = x.shape
    tm = min(MLP_TM, seq)
    tile = pl.BlockSpec((1, tm, d), lambda b, s: (b, s, 0))
    return pl.pallas_call(
        functools.partial(_mlp_kernel, alpha),
        grid=(batch, seq // tm),
        in_specs=[tile,
                  pl.BlockSpec((1, 6, d), lambda b, s: (b, 0, 0)),
                  _resident(wup.shape), _resident(bup.shape), _resident(wdn.shape),
                  _resident(bdn.shape), _resident(ln_g.shape), _resident(ln_b.shape)],
        out_specs=tile,
        out_shape=jax.ShapeDtypeStruct(x.shape, F32),
        scratch_shapes=[pltpu.VMEM((tm, D_FF), BF16)],
        compiler_params=pltpu.CompilerParams(
            dimension_semantics=("arbitrary", "arbitrary"), vmem_limit_bytes=VMEM_LIMIT),
        name="mlp_sublayer",
    )(x, mod, wup, bup, wdn, bdn, ln_g, ln_b)


def _split_in_projection(w_in):
    d = w_in.shape[0]
    hg = w_in[:, 0:4 * D_MODEL].reshape(d, 4, HG_PAIRS, 2 * HG_D)
    wh = hg.transpose(2, 0, 1, 3).reshape(HG_PAIRS, d, 4 * 2 * HG_D)
    off = 4 * D_MODEL

    def qk(start):
        t = w_in[:, start:start + RET_HEADS * RET_DK].reshape(d, RET_HEADS, RET_DK // 2, 2)
        return jnp.concatenate([t[..., 0], t[..., 1]], axis=-1)

    rq = qk(off)
    rk = qk(off + RET_HEADS * RET_DK)
    off += 2 * RET_HEADS * RET_DK
    rv = w_in[:, off:off + RET_HEADS * RET_DV].reshape(d, RET_HEADS, RET_DV)
    off += RET_HEADS * RET_DV
    rg = w_in[:, off:off + RET_HEADS * RET_DV].reshape(d, RET_HEADS, RET_DV)
    off += RET_HEADS * RET_DV
    wr = jnp.concatenate([rq, rk, rv, rg], axis=-1).transpose(1, 0, 2)
    wg = w_in[:, off:]
    return wh.astype(BF16), wr.astype(BF16), wg.astype(BF16)


def kernel(x, c, positions, lb_logits, w_ada, b_ada, w_in, w_pa, w_pb, w_o, ln1_g, ln1_b,
           w_up, b_up, w_down, b_down, ln2_g, ln2_b):
    depth = w_in.shape[0]
    batch, seq, d = x.shape
    alpha = (2 * depth) ** 0.25
    mod = _modulation(c, w_ada, b_ada).reshape(depth, batch, 6, d)
    lbt = _lower_bound_tables(lb_logits)
    lbt = lbt.reshape(depth, 8, HG_HEADS, HG_D).transpose(0, 2, 1, 3)
    theta = (ROPE_BASE ** (-jnp.linspace(0.0, 1.0, RET_DK // 2, dtype=F32))).reshape(1, -1)
    log_gamma = jnp.log1p(-jnp.exp2(-5.0 - jnp.arange(RET_HEADS, dtype=F32)))
    rtab = jnp.broadcast_to(log_gamma[:, None, None], (RET_HEADS, 8, RET_DV))
    pos = positions.reshape(batch, seq, 1)
    for l in range(depth):
        wh, wr, wg = _split_in_projection(w_in[l])
        x = _mixer(alpha, x, mod[l], pos, theta, lbt[l], rtab, wh, wr, wg,
                   w_pa[l].astype(BF16).reshape(HG_PAIRS, 2 * HG_D, d),
                   w_pb[l].astype(BF16).reshape(RET_HEADS, RET_DV, d),
                   w_o[l].astype(BF16), ln1_g[l].reshape(1, d), ln1_b[l].reshape(1, d))
        x = _mlp(alpha, x, mod[l], w_up[l].astype(BF16), b_up[l].reshape(1, -1),
                 w_down[l].astype(BF16), b_down[l].reshape(1, d),
                 ln2_g[l].reshape(1, d), ln2_b[l].reshape(1, d))
    return x
```

```python
import functools
import math

import jax
import jax.numpy as jnp
from jax import lax
from jax.experimental import pallas as pl
from jax.experimental.pallas import tpu as pltpu

F32 = jnp.float32
BF16 = jnp.bfloat16

D_MODEL = 1024
HG_HEADS = 8
SUBLANES = 8
HG_D = 128
HG_PAIRS = HG_HEADS // 2
RET_HEADS = 4
RET_DK = 256
RET_DV = 512
RET_W = 2 * RET_DK + 2 * RET_DV
D_FF = 4 * D_MODEL
ROPE_BASE = 10000.0
LN_EPS = 1e-5
HEAD_NORM_EPS = 1e-6

MIX_TM = 256
HG_CHUNK = 128
MLP_TM = 512
ROPE_ROWS = 1024
VMEM_LIMIT = 60 * 1024 * 1024

NT_DIMS = (((1,), (1,)), ((), ()))
TN_DIMS = (((0,), (0,)), ((), ()))


def _resident(shape):
    n = len(shape)
    return pl.BlockSpec(shape, lambda *_: (0,) * n, pipeline_mode=pl.Buffered(1))


def _layer_norm(r, g, b):
    mu = jnp.mean(r, axis=-1, keepdims=True)
    d = r - mu
    var = jnp.mean(d * d, axis=-1, keepdims=True)
    return d * lax.rsqrt(var + LN_EPS) * g + b


def _silu(z):
    return z * jax.nn.sigmoid(z)


def _mod_kernel(c_ref, w_ref, b_ref, o_ref):
    cond = _silu(c_ref[...])
    o_ref[0] = jnp.dot(cond, w_ref[0], preferred_element_type=F32,
                       precision=lax.Precision.HIGHEST) + b_ref[0]


def _modulation(c, w_ada, b_ada):
    depth, d, n = w_ada.shape
    batch = c.shape[0]
    nb = n // d
    return pl.pallas_call(
        _mod_kernel,
        grid=(depth, nb),
        in_specs=[pl.BlockSpec((batch, d), lambda l, j: (0, 0)),
                  pl.BlockSpec((1, d, d), lambda l, j: (l, 0, j)),
                  pl.BlockSpec((1, 1, d), lambda l, j: (l, 0, j))],
        out_specs=pl.BlockSpec((1, batch, d), lambda l, j: (l, 0, j)),
        out_shape=jax.ShapeDtypeStruct((depth, batch, n), F32),
        name="ada_modulation",
    )(c, w_ada, b_ada.reshape(depth, 1, n))


def _rope_kernel(pos_ref, theta_ref, o_ref):
    ang = pos_ref[0].astype(F32) * theta_ref[...]
    half = RET_DK // 2
    o_ref[0, :, 0:half] = jnp.cos(ang)
    o_ref[0, :, half:RET_DK] = jnp.sin(ang)


def _rope_table(positions, theta):
    batch, seq = positions.shape
    rows = min(ROPE_ROWS, seq)
    return pl.pallas_call(
        _rope_kernel,
        grid=(batch, seq // rows),
        in_specs=[pl.BlockSpec((1, rows, 1), lambda b, s: (b, s, 0)),
                  pl.BlockSpec(theta.shape, lambda b, s: (0, 0))],
        out_specs=pl.BlockSpec((1, rows, RET_DK), lambda b, s: (b, s, 0)),
        out_shape=jax.ShapeDtypeStruct((batch, seq, RET_DK), F32),
        name="rope_table",
    )(positions.reshape(batch, seq, 1), theta)


def _lb_kernel(depth, logit_ref, o_ref):
    z = logit_ref[...]
    e = jnp.exp(z - jnp.max(z, axis=0, keepdims=True))
    sm = e / jnp.sum(e, axis=0, keepdims=True)
    o_ref[...] = jnp.zeros_like(o_ref)
    cum = jnp.zeros_like(sm[0:1])
    for l in range(depth):
        cum = cum + sm[l:l + 1]
        lb = cum - sm[0:1]
        o_ref[l, 0:1, :] = jnp.log(lb)
        o_ref[l, 1:2, :] = jnp.log1p(-lb)
        o_ref[l, 2:3, :] = 1.0 - lb


def _lower_bound_tables(lb_logits):
    depth, n = lb_logits.shape
    return pl.pallas_call(
        functools.partial(_lb_kernel, depth),
        out_shape=jax.ShapeDtypeStruct((depth, 8, n), F32),
        name="hgrn_lower_bounds",
    )(lb_logits.astype(F32))


def _cumsum_rows(x):
    c = x.shape[0]
    head_rows = lax.broadcasted_iota(jnp.int32, (SUBLANES, HG_D), 0)
    shift = 1
    while shift < c:
        if shift < SUBLANES:
            moved = pltpu.roll(x, shift, 0)
            first = jnp.where(head_rows >= shift, moved[0:SUBLANES], 0.0)
            moved = jnp.concatenate([first, moved[SUBLANES:]], axis=0)
        else:
            moved = jnp.concatenate([jnp.zeros((shift, HG_D), F32), x[:c - shift]], axis=0)
        x = x + moved
        shift *= 2
    return x


def _level_operands(q, k, b, b_ref, level):
    c = b.shape[0]
    half = 1 << level
    if half >= SUBLANES:
        sel, rho = [], []
        for p in range(c // half):
            mid = (p // 2) * 2 * half + half - 1
            sel.append((q if p % 2 else k)[p * half:(p + 1) * half])
            rho.append(jnp.broadcast_to(b_ref[mid:mid + 1, :], (half, HG_D)))
        return jnp.concatenate(sel, axis=0), jnp.concatenate(rho, axis=0)
    shape3 = (c // SUBLANES, SUBLANES, HG_D)
    b3 = b.reshape(shape3)
    sub = lax.broadcasted_iota(jnp.int32, (1, SUBLANES, HG_D), 1)
    rho = None
    for p in reversed(range(SUBLANES // (2 * half))):
        mid = p * 2 * half + half - 1
        piece = jnp.broadcast_to(b3[:, mid:mid + 1, :], shape3)
        rho = piece if rho is None else jnp.where(sub < (p + 1) * 2 * half, piece, rho)
    sel = jnp.where(((sub >> level) & 1) == 1, q.reshape(shape3), k.reshape(shape3))
    return sel.reshape(c, HG_D), rho.reshape(c, HG_D)


def _hgrn_chunk(q, lf, k, v, st_ref, head, b_ref):
    c = q.shape[0]
    b = _cumsum_rows(lf)
    b_ref[...] = b
    ri = lax.broadcasted_iota(jnp.int32, (c, c), 0)
    ci = lax.broadcasted_iota(jnp.int32, (c, c), 1)
    pair_level = jnp.where(ri > ci, 31 - lax.clz(ri ^ ci), -1)
    scores = jnp.zeros((c, c), F32)
    for level in range(c.bit_length() - 1):
        sel, rho = _level_operands(q, k, b, b_ref, level)
        z = (sel * jnp.exp(-jnp.abs(b - rho))).astype(BF16)
        s_level = lax.dot_general(z, z, NT_DIMS, preferred_element_type=F32)
        scores = jnp.where(pair_level == level, s_level, scores)
    diag = jnp.sum(q * k, axis=-1, keepdims=True)
    v16 = v.astype(BF16)
    o = jnp.dot(scores.astype(BF16), v16, preferred_element_type=F32) + diag * v
    st = st_ref[head]
    o = o + lax.dot_general((q * jnp.exp(b)).astype(BF16), st.astype(BF16), NT_DIMS,
                            preferred_element_type=F32)
    b_last = b_ref[c - 1:c, :]
    k_dec = (k * jnp.exp(b_last - b)).astype(BF16)
    st_ref[head] = st * jnp.exp(b_last) + lax.dot_general(v16, k_dec, TN_DIMS,
                                                          preferred_element_type=F32)
    return o


def _hgrn_pair(p, u_ref, wh_ref, lbt_ref, proj_ref, oa_ref, st_ref, b_ref):
    tm = u_ref.shape[0]
    proj_ref[:, 0:4 * 2 * HG_D] = jnp.dot(u_ref[...], wh_ref[p], preferred_element_type=F32)
    for j in range(2):
        head = 2 * p + j
        tab = lbt_ref[head]
        log_lb, log1m_lb, one_m_lb = tab[0:1], tab[1:2], tab[2:3]
        for ck in range(tm // HG_CHUNK):
            rows = slice(ck * HG_CHUNK, (ck + 1) * HG_CHUNK)
            zq = proj_ref[rows, j * HG_D:(j + 1) * HG_D]
            zf = proj_ref[rows, (2 + j) * HG_D:(3 + j) * HG_D]
            v = proj_ref[rows, (4 + j) * HG_D:(5 + j) * HG_D]
            zg = proj_ref[rows, (6 + j) * HG_D:(7 + j) * HG_D]
            e = jnp.exp(-jnp.abs(zf))
            one_p_e = 1.0 + e
            log_sig = jnp.minimum(zf, 0.0) - jnp.log(one_p_e)
            k = one_m_lb * (jnp.where(zf >= 0.0, e, 1.0) / one_p_e)
            a2 = log1m_lb + log_sig
            lf = jnp.maximum(log_lb, a2) + jnp.log(1.0 + jnp.exp(-jnp.abs(log_lb - a2)))
            o = _hgrn_chunk(_silu(zq), lf, k, v, st_ref, head,
                            b_ref.at[j * (tm // HG_CHUNK) + ck])
            o = o * lax.rsqrt(jnp.mean(o * o, axis=-1, keepdims=True) + HEAD_NORM_EPS)
            oa_ref[p, rows, j * HG_D:(j + 1) * HG_D] = (o * _silu(zg)).astype(BF16)


def _retention_head(h, u_ref, wr_ref, rtab_ref, cs_ref, proj_ref, ob_ref, st_ref):
    c = u_ref.shape[0]
    proj_ref[...] = jnp.dot(u_ref[...], wr_ref[h], preferred_element_type=F32)
    half = RET_DK // 2
    cos, sin = cs_ref[0, :, 0:half], cs_ref[0, :, half:RET_DK]

    def rotate(off):
        t1 = proj_ref[:, off:off + half]
        t2 = proj_ref[:, off + half:off + RET_DK]
        return jnp.concatenate([t1 * cos - t2 * sin, t1 * sin + t2 * cos], axis=1)

    q = rotate(0)
    k = rotate(RET_DK) * (RET_DK ** -0.5)
    v16 = proj_ref[:, 2 * RET_DK:2 * RET_DK + RET_DV].astype(BF16)
    log_gamma_wide = rtab_ref[h][0:1, :]
    log_gamma = log_gamma_wide[:, 0:RET_DK]
    ri = lax.broadcasted_iota(jnp.int32, (c, c), 0)
    ci = lax.broadcasted_iota(jnp.int32, (c, c), 1)
    diff = (ri - ci).astype(F32)
    decay = jnp.where(diff >= 0.0, jnp.exp(log_gamma * jnp.maximum(diff, 0.0)), 0.0)
    pos = lax.broadcasted_iota(jnp.int32, (c, RET_DK), 0).astype(F32)
    q_decay = jnp.exp(log_gamma * (pos + 1.0))
    k_decay = jnp.exp(log_gamma * (c - 1.0 - pos))
    scores = lax.dot_general(q.astype(BF16), k.astype(BF16), NT_DIMS,
                             preferred_element_type=F32) * decay
    st = st_ref[h]
    o = (jnp.dot(scores.astype(BF16), v16, preferred_element_type=F32)
         + jnp.dot((q * q_decay).astype(BF16), st.astype(BF16), preferred_element_type=F32))
    chunk_decay = jnp.exp(log_gamma_wide * float(c))
    st_ref[h] = chunk_decay * st + lax.dot_general((k * k_decay).astype(BF16), v16, TN_DIMS,
                                                   preferred_element_type=F32)
    o = o - jnp.mean(o, axis=-1, keepdims=True)
    o = o * lax.rsqrt(jnp.mean(o * o, axis=-1, keepdims=True) + HEAD_NORM_EPS)
    g = proj_ref[:, 2 * RET_DK + RET_DV:RET_W]
    ob_ref[h] = (o * _silu(g)).astype(BF16)


def _mixer_kernel(alpha, x_ref, mod_ref, cs_ref, lbt_ref, rtab_ref,
                  wh_ref, wr_ref, wg_ref, wpa_ref, wpb_ref, wo_ref, lng_ref, lnb_ref,
                  o_ref, u_ref, projh_ref, projr_ref, oa_ref, ob_ref, sth_ref, str_ref, b_ref):
    @pl.when(pl.program_id(1) == 0)
    def _():
        sth_ref[...] = jnp.zeros_like(sth_ref)
        str_ref[...] = jnp.zeros_like(str_ref)

    x = x_ref[0]
    shift, scale, gate = mod_ref[0, 0:1, :], mod_ref[0, 1:2, :], mod_ref[0, 2:3, :]
    u_ref[...] = (x * (1.0 + scale) + shift).astype(BF16)

    assert HG_PAIRS == RET_HEADS

    def group_body(i, carry):
        _hgrn_pair(i, u_ref, wh_ref, lbt_ref, projh_ref, oa_ref, sth_ref, b_ref)
        _retention_head(i, u_ref, wr_ref, rtab_ref, cs_ref, projr_ref, ob_ref, str_ref)
        return carry

    lax.fori_loop(0, HG_PAIRS, group_body, 0, unroll=True)

    ya = jnp.dot(oa_ref[0], wpa_ref[0], preferred_element_type=F32)
    for p in range(1, HG_PAIRS):
        ya = ya + jnp.dot(oa_ref[p], wpa_ref[p], preferred_element_type=F32)
    yb = jnp.dot(ob_ref[0], wpb_ref[0], preferred_element_type=F32)
    for h in range(1, RET_HEADS):
        yb = yb + jnp.dot(ob_ref[h], wpb_ref[h], preferred_element_type=F32)
    gates = jnp.dot(u_ref[...], wg_ref[...], preferred_element_type=F32)
    y = (jax.nn.sigmoid(gates[:, 0:D_MODEL]) * ya
         + jax.nn.sigmoid(gates[:, D_MODEL:2 * D_MODEL]) * yb)
    y = jnp.dot(y.astype(BF16), wo_ref[...], preferred_element_type=F32)
    o_ref[0] = _layer_norm(alpha * x + (1.0 + gate) * y, lng_ref[...], lnb_ref[...])


def _mixer(alpha, x, mod, cos_sin, lbt, rtab, wh, wr, wg, wpa, wpb, wo, ln_g, ln_b):
    batch, seq, d = x.shape
    tm = min(MIX_TM, seq)
    tile = pl.BlockSpec((1, tm, d), lambda b, s: (b, s, 0))
    return pl.pallas_call(
        functools.partial(_mixer_kernel, alpha),
        grid=(batch, seq // tm),
        in_specs=[tile,
                  pl.BlockSpec((1, 6, d), lambda b, s: (b, 0, 0)),
                  pl.BlockSpec((1, tm, RET_DK), lambda b, s: (b, s, 0)),
                  _resident(lbt.shape), _resident(rtab.shape),
                  _resident(wh.shape), _resident(wr.shape), _resident(wg.shape),
                  _resident(wpa.shape), _resident(wpb.shape), _resident(wo.shape),
                  _resident(ln_g.shape), _resident(ln_b.shape)],
        out_specs=tile,
        out_shape=jax.ShapeDtypeStruct(x.shape, F32),
        scratch_shapes=[pltpu.VMEM((tm, d), BF16),
                        pltpu.VMEM((tm, 4 * 2 * HG_D), F32),
                        pltpu.VMEM((tm, RET_W), F32),
                        pltpu.VMEM((HG_PAIRS, tm, 2 * HG_D), BF16),
                        pltpu.VMEM((RET_HEADS, tm, RET_DV), BF16),
                        pltpu.VMEM((HG_HEADS, HG_D, HG_D), F32),
                        pltpu.VMEM((RET_HEADS, RET_DK, RET_DV), F32),
                        pltpu.VMEM((2 * (tm // HG_CHUNK), HG_CHUNK, HG_D), F32)],
        compiler_params=pltpu.CompilerParams(
            dimension_semantics=("arbitrary", "arbitrary"), vmem_limit_bytes=VMEM_LIMIT),
        name="mixer_sublayer",
    )(x, mod, cos_sin, lbt, rtab, wh, wr, wg, wpa, wpb, wo, ln_g, ln_b)


def _mlp_kernel(alpha, x_ref, mod_ref, wup_ref, bup_ref, wdn_ref, bdn_ref, lng_ref, lnb_ref,
                o_ref, h_ref):
    x = x_ref[0]
    shift, scale, gate = mod_ref[0, 3:4, :], mod_ref[0, 4:5, :], mod_ref[0, 5:6, :]
    u = (x * (1.0 + scale) + shift).astype(BF16)
    for j in range(D_FF // D_MODEL):
        cols = slice(j * D_MODEL, (j + 1) * D_MODEL)
        hj = jnp.dot(u, wup_ref[:, cols], preferred_element_type=F32) + bup_ref[:, cols]
        h_ref[:, cols] = jnp.square(jnp.maximum(hj, 0.0)).astype(BF16)
    y = jnp.dot(h_ref[...], wdn_ref[...], preferred_element_type=F32) + bdn_ref[...]
    o_ref[0] = _layer_norm(alpha * x + (1.0 + gate) * y, lng_ref[...], lnb_ref[...])


def _mlp(alpha, x, mod, wup, bup, wdn, bdn, ln_g, ln_b):
    batch, seq, d = x.shape
    tm = min(MLP_TM, seq)
    tile = pl.BlockSpec((1, tm, d), lambda b, s: (b, s, 0))
    return pl.pallas_call(
        functools.partial(_mlp_kernel, alpha),
        grid=(batch, seq // tm),
        in_specs=[tile,
                  pl.BlockSpec((1, 6, d), lambda b, s: (b, 0, 0)),
                  _resident(wup.shape), _resident(bup.shape), _resident(wdn.shape),
                  _resident(bdn.shape), _resident(ln_g.shape), _resident(ln_b.shape)],
        out_specs=tile,
        out_shape=jax.ShapeDtypeStruct(x.shape, F32),
        scratch_shapes=[pltpu.VMEM((tm, D_FF), BF16)],
        compiler_params=pltpu.CompilerParams(
            dimension_semantics=("arbitrary", "arbitrary"), vmem_limit_bytes=VMEM_LIMIT),
        name="mlp_sublayer",
    )(x, mod, wup, bup, wdn, bdn, ln_g, ln_b)


def _split_in_projection(w_in):
    d = w_in.shape[0]
    hg = w_in[:, 0:4 * D_MODEL].reshape(d, 4, HG_PAIRS, 2 * HG_D)
    wh = hg.transpose(2, 0, 1, 3).reshape(HG_PAIRS, d, 4 * 2 * HG_D)
    off = 4 * D_MODEL

    def qk(start):
        t = w_in[:, start:start + RET_HEADS * RET_DK].reshape(d, RET_HEADS, RET_DK // 2, 2)
        return jnp.concatenate([t[..., 0], t[..., 1]], axis=-1)

    rq = qk(off)
    rk = qk(off + RET_HEADS * RET_DK)
    off += 2 * RET_HEADS * RET_DK
    rv = w_in[:, off:off + RET_HEADS * RET_DV].reshape(d, RET_HEADS, RET_DV)
    off += RET_HEADS * RET_DV
    rg = w_in[:, off:off + RET_HEADS * RET_DV].reshape(d, RET_HEADS, RET_DV)
    off += RET_HEADS * RET_DV
    wr = jnp.concatenate([rq, rk, rv, rg], axis=-1).transpose(1, 0, 2)
    wg = w_in[:, off:]
    return wh.astype(BF16), wr.astype(BF16), wg.astype(BF16)


def kernel(x, c, positions, lb_logits, w_ada, b_ada, w_in, w_pa, w_pb, w_o, ln1_g, ln1_b,
           w_up, b_up, w_down, b_down, ln2_g, ln2_b):
    depth = w_in.shape[0]
    batch, seq, d = x.shape
    alpha = (2 * depth) ** 0.25
    mod = _modulation(c, w_ada, b_ada).reshape(depth, batch, 6, d)
    lbt = _lower_bound_tables(lb_logits)
    lbt = lbt.reshape(depth, 8, HG_HEADS, HG_D).transpose(0, 2, 1, 3)
    theta = (ROPE_BASE ** (-jnp.linspace(0.0, 1.0, RET_DK // 2, dtype=F32))).reshape(1, -1)
    log_gamma = jnp.log1p(-jnp.exp2(-5.0 - jnp.arange(RET_HEADS, dtype=F32)))
    rtab = jnp.broadcast_to(log_gamma[:, None, None], (RET_HEADS, 8, RET_DV))
    cos_sin = _rope_table(positions, theta)
    for l in range(depth):
        wh, wr, wg = _split_in_projection(w_in[l])
        x = _mixer(alpha, x, mod[l], cos_sin, lbt[l], rtab, wh, wr, wg,
                   w_pa[l].astype(BF16).reshape(HG_PAIRS, 2 * HG_D, d),
                   w_pb[l].astype(BF16).reshape(RET_HEADS, RET_DV, d),
                   w_o[l].astype(BF16), ln1_g[l].reshape(1, d), ln1_b[l].reshape(1, d))
        x = _mlp(alpha, x, mod[l], w_up[l].astype(BF16), b_up[l].reshape(1, -1),
                 w_down[l].astype(BF16), b_down[l].reshape(1, d),
                 ln2_g[l].reshape(1, d), ln2_b[l].reshape(1, d))
    return x
```

```python
import functools

import jax
import jax.numpy as jnp
import numpy as np
from jax import lax
from jax.experimental import pallas as pl
from jax.experimental.pallas import tpu as pltpu

F32 = jnp.float32
BF16 = jnp.bfloat16

D_MODEL = 1024
HG_HEADS = 8
SUBLANES = 8
HG_D = 128
HG_PAIRS = HG_HEADS // 2
RET_HEADS = 4
RET_DK = 256
RET_DV = 512
W_BLOCK = 256
BLK_HG_Q, BLK_HG_F, BLK_HG_V, BLK_HG_G = 0, 4, 8, 12
BLK_RET_Q, BLK_RET_K, BLK_RET_V, BLK_RET_G = 16, 20, 24, 32
BLK_GATES = 40
N_BLOCKS = 48
D_FF = 4 * D_MODEL
ROPE_BASE = 10000.0
LN_EPS = 1e-5
HEAD_NORM_EPS = 1e-6
LOG2_E = 1.4426950408889634

MIX_TM = 256
HG_CHUNK = 128
MLP_TM = 512
ROPE_ROWS = 1024
VMEM_LIMIT = 60 * 1024 * 1024

NT_DIMS = (((1,), (1,)), ((), ()))
TN_DIMS = (((0,), (0,)), ((), ()))


def _resident(shape):
    n = len(shape)
    return pl.BlockSpec(shape, lambda *_: (0,) * n, pipeline_mode=pl.Buffered(1))


def _layer_norm(r, g, b):
    mu = jnp.mean(r, axis=-1, keepdims=True)
    d = r - mu
    var = jnp.mean(d * d, axis=-1, keepdims=True)
    return d * lax.rsqrt(var + LN_EPS) * g + b


def _silu(z):
    return z * jax.nn.sigmoid(z)


def _mod_kernel(c_ref, w_ref, b_ref, o_ref):
    cond = _silu(c_ref[...])
    o_ref[0] = jnp.dot(cond, w_ref[0], preferred_element_type=F32,
                       precision=lax.Precision.HIGHEST) + b_ref[0]


def _modulation(c, w_ada, b_ada):
    depth, d, n = w_ada.shape
    batch = c.shape[0]
    nb = n // d
    return pl.pallas_call(
        _mod_kernel,
        grid=(depth, nb),
        in_specs=[pl.BlockSpec((batch, d), lambda l, j: (0, 0)),
                  pl.BlockSpec((1, d, d), lambda l, j: (l, 0, j)),
                  pl.BlockSpec((1, 1, d), lambda l, j: (l, 0, j))],
        out_specs=pl.BlockSpec((1, batch, d), lambda l, j: (l, 0, j)),
        out_shape=jax.ShapeDtypeStruct((depth, batch, n), F32),
        name="ada_modulation",
    )(c, w_ada, b_ada.reshape(depth, 1, n))


def _rope_kernel(pos_ref, theta_ref, o_ref):
    ang = pos_ref[0].astype(F32) * theta_ref[...]
    half = RET_DK // 2
    o_ref[0, :, 0:half] = jnp.cos(ang)
    o_ref[0, :, half:RET_DK] = jnp.sin(ang)


def _rope_table(positions, theta):
    batch, seq = positions.shape
    rows = min(ROPE_ROWS, seq)
    return pl.pallas_call(
        _rope_kernel,
        grid=(batch, seq // rows),
        in_specs=[pl.BlockSpec((1, rows, 1), lambda b, s: (b, s, 0)),
                  pl.BlockSpec(theta.shape, lambda b, s: (0, 0))],
        out_specs=pl.BlockSpec((1, rows, RET_DK), lambda b, s: (b, s, 0)),
        out_shape=jax.ShapeDtypeStruct((batch, seq, RET_DK), F32),
        name="rope_table",
    )(positions.reshape(batch, seq, 1), theta)


def _lb_kernel(depth, logit_ref, o_ref):
    z = logit_ref[...]
    e = jnp.exp(z - jnp.max(z, axis=0, keepdims=True))
    sm = e / jnp.sum(e, axis=0, keepdims=True)
    o_ref[...] = jnp.zeros_like(o_ref)
    cum = jnp.zeros_like(sm[0:1])
    for l in range(depth):
        cum = cum + sm[l:l + 1]
        lb = cum - sm[0:1]
        o_ref[l, 0:1, :] = jnp.log(lb)
        o_ref[l, 1:2, :] = jnp.log1p(-lb)
        o_ref[l, 2:3, :] = 1.0 - lb


def _lower_bound_tables(lb_logits):
    depth, n = lb_logits.shape
    return pl.pallas_call(
        functools.partial(_lb_kernel, depth),
        out_shape=jax.ShapeDtypeStruct((depth, 8, n), F32),
        name="hgrn_lower_bounds",
    )(lb_logits.astype(F32))


def _regroup_kernel(w_ref, o_ref):
    j = pl.program_id(1)
    blk = w_ref[0].astype(BF16)
    rotary = (j >= BLK_RET_Q) & (j < BLK_RET_V)

    @pl.when(rotary)
    def _():
        src = lax.broadcasted_iota(jnp.int32, (W_BLOCK, W_BLOCK), 0)
        dst = lax.broadcasted_iota(jnp.int32, (W_BLOCK, W_BLOCK), 1)
        half = W_BLOCK // 2
        want = jnp.where(dst < half, 2 * dst, 2 * (dst - half) + 1)
        perm = (src == want).astype(BF16)
        o_ref[0, 0] = jnp.dot(blk, perm, preferred_element_type=F32).astype(BF16)

    @pl.when(jnp.logical_not(rotary))
    def _():
        o_ref[0, 0] = blk


def _regroup_in_projection(w_in):
    depth, d, n = w_in.shape
    assert n == N_BLOCKS * W_BLOCK and RET_DK == W_BLOCK
    return pl.pallas_call(
        _regroup_kernel,
        grid=(depth, N_BLOCKS),
        in_specs=[pl.BlockSpec((1, d, W_BLOCK), lambda l, j: (l, 0, j))],
        out_specs=pl.BlockSpec((1, 1, d, W_BLOCK), lambda l, j: (l, j, 0, 0)),
        out_shape=jax.ShapeDtypeStruct((depth, N_BLOCKS, d, W_BLOCK), BF16),
        name="regroup_in_projection",
    )(w_in)


def _cumsum_rows(x):
    c = x.shape[0]
    head_rows = lax.broadcasted_iota(jnp.int32, (SUBLANES, HG_D), 0)
    shift = 1
    while shift < c:
        if shift < SUBLANES:
            moved = pltpu.roll(x, shift, 0)
            first = jnp.where(head_rows >= shift, moved[0:SUBLANES], 0.0)
            moved = jnp.concatenate([first, moved[SUBLANES:]], axis=0)
        else:
            moved = jnp.concatenate([jnp.zeros((shift, HG_D), F32), x[:c - shift]], axis=0)
        x = x + moved
        shift *= 2
    return x


def _neg_abs(x):
    bits = lax.bitcast_convert_type(x, jnp.uint32) | jnp.uint32(0x80000000)
    return lax.bitcast_convert_type(bits, F32)


def _level_operands(q, k, b, b_ref, level):
    c = b.shape[0]
    half = 1 << level
    if half >= SUBLANES:
        sel, expo = [], []
        for p in range(c // half):
            rows = slice(p * half, (p + 1) * half)
            mid = (p // 2) * 2 * half + half - 1
            rho = jnp.broadcast_to(b_ref[mid:mid + 1, :], (half, HG_D))
            sel.append((q if p % 2 else k)[rows])
            expo.append(b[rows] - rho if p % 2 else rho - b[rows])
        return jnp.concatenate(sel, axis=0), jnp.exp2(jnp.concatenate(expo, axis=0))
    shape3 = (c // SUBLANES, SUBLANES, HG_D)
    b3 = b.reshape(shape3)
    sub = lax.broadcasted_iota(jnp.int32, (1, SUBLANES, HG_D), 1)
    rho = None
    for p in reversed(range(SUBLANES // (2 * half))):
        mid = p * 2 * half + half - 1
        piece = jnp.broadcast_to(b3[:, mid:mid + 1, :], shape3)
        rho = piece if rho is None else jnp.where(sub < (p + 1) * 2 * half, piece, rho)
    sel = jnp.where(((sub >> level) & 1) == 1, q.reshape(shape3), k.reshape(shape3))
    return sel.reshape(c, HG_D), jnp.exp2(_neg_abs(b3 - rho)).reshape(c, HG_D)


def _hgrn_chunk(q, lf2, k, v, st_ref, head, lvl_ref, b_ref):
    c = q.shape[0]
    b = _cumsum_rows(lf2)
    b_ref[...] = b
    scores = [jnp.zeros((SUBLANES, c), F32)] * (c // SUBLANES)
    for level in range(c.bit_length() - 1):
        sel, dec = _level_operands(q, k, b, b_ref, level)
        z = (sel * dec).astype(BF16)
        s_level = lax.dot_general(z, z, NT_DIMS, preferred_element_type=F32)
        for r in range(c // SUBLANES):
            if (r * SUBLANES) >> level & 1 or (1 << level) < SUBLANES:
                rows = slice(r * SUBLANES, (r + 1) * SUBLANES)
                scores[r] = jnp.where(lvl_ref[rows, :] == level, s_level[rows], scores[r])
    scores = jnp.concatenate(scores, axis=0)
    diag = jnp.sum(q * k, axis=-1, keepdims=True)
    v16 = v.astype(BF16)
    o = jnp.dot(scores.astype(BF16), v16, preferred_element_type=F32) + diag * v
    st = st_ref[head]
    o = o + lax.dot_general((q * jnp.exp2(b)).astype(BF16), st.astype(BF16), NT_DIMS,
                            preferred_element_type=F32)
    b_last = b_ref[c - 1:c, :]
    k_dec = (k * jnp.exp2(b_last - b)).astype(BF16)
    st_ref[head] = st * jnp.exp2(b_last) + lax.dot_general(v16, k_dec, TN_DIMS,
                                                           preferred_element_type=F32)
    return o


def _project(u, w_ref, first, count=1):
    parts = [jnp.dot(u, w_ref[first + t], preferred_element_type=F32) for t in range(count)]
    return parts[0] if count == 1 else jnp.concatenate(parts, axis=1)


def _hgrn_project(p, slot, u_ref, w_ref, lbt_ref, q_ref, lf_ref, k_ref, v_ref, g_ref):
    u = u_ref[...]
    tab = jnp.concatenate([lbt_ref[2 * p], lbt_ref[2 * p + 1]], axis=1)
    log_lb, log1m_lb, one_m_lb = tab[0:1], tab[1:2], tab[2:3]
    q_ref[slot] = _silu(_project(u, w_ref, BLK_HG_Q + p))
    zf = _project(u, w_ref, BLK_HG_F + p)
    e = jnp.exp(-jnp.abs(zf))
    one_p_e = 1.0 + e
    a2 = log1m_lb + jnp.minimum(zf, 0.0) - jnp.log(one_p_e)
    lf = jnp.maximum(log_lb, a2) + jnp.log(1.0 + jnp.exp(-jnp.abs(log_lb - a2)))
    lf_ref[slot] = lf * LOG2_E
    k_ref[slot] = one_m_lb * (jnp.where(zf >= 0.0, e, 1.0) / one_p_e)
    v_ref[slot] = _project(u, w_ref, BLK_HG_V + p)
    g_ref[slot] = _silu(_project(u, w_ref, BLK_HG_G + p))


def _hgrn_compute(p, slot, lvl_ref, q_ref, lf_ref, k_ref, v_ref, g_ref, oa_ref, st_ref, b_ref):
    tm = q_ref.shape[1]
    for j in range(2):
        cols = slice(j * HG_D, (j + 1) * HG_D)
        for ck in range(tm // HG_CHUNK):
            rows = slice(ck * HG_CHUNK, (ck + 1) * HG_CHUNK)
            o = _hgrn_chunk(q_ref[slot, rows, cols], lf_ref[slot, rows, cols],
                            k_ref[slot, rows, cols], v_ref[slot, rows, cols],
                            st_ref, 2 * p + j, lvl_ref, b_ref.at[j * (tm // HG_CHUNK) + ck])
            o = o * lax.rsqrt(jnp.mean(o * o, axis=-1, keepdims=True) + HEAD_NORM_EPS)
            oa_ref[p, rows, cols] = (o * g_ref[slot, rows, cols]).astype(BF16)


def _retention_project(h, slot, u_ref, w_ref, cs_ref, qdec_ref, kdec_ref,
                       q_ref, qd_ref, k_ref, kd_ref, v_ref, g_ref):
    half = RET_DK // 2
    u = u_ref[...]
    cos, sin = cs_ref[0, :, 0:half], cs_ref[0, :, half:RET_DK]

    def rotated(block, dec_ref, plain_ref, scaled_ref):
        t = _project(u, w_ref, block)
        t1, t2 = t[:, 0:half], t[:, half:RET_DK]
        r1, r2 = t1 * cos - t2 * sin, t1 * sin + t2 * cos
        dec = dec_ref[h]
        plain_ref[slot] = jnp.concatenate([r1, r2], axis=1).astype(BF16)
        scaled_ref[slot] = jnp.concatenate([r1 * dec, r2 * dec], axis=1).astype(BF16)

    rotated(BLK_RET_Q + h, qdec_ref, q_ref, qd_ref)
    rotated(BLK_RET_K + h, kdec_ref, k_ref, kd_ref)
    wide = RET_DV // W_BLOCK
    v_ref[slot] = _project(u, w_ref, BLK_RET_V + wide * h, wide).astype(BF16)
    g_ref[slot] = _silu(_project(u, w_ref, BLK_RET_G + wide * h, wide))


def _retention_compute(h, slot, dmask_ref, cdec_ref, q_ref, qd_ref, k_ref, kd_ref, v_ref, g_ref,
                       ob_ref, st_ref):
    v16 = v_ref[slot]
    scores = lax.dot_general(q_ref[slot], k_ref[slot], NT_DIMS,
                             preferred_element_type=F32) * dmask_ref[h]
    st = st_ref[h]
    o = (jnp.dot(scores.astype(BF16), v16, preferred_element_type=F32)
         + jnp.dot(qd_ref[slot], st.astype(BF16), preferred_element_type=F32))
    st_ref[h] = cdec_ref[h][0:1, :] * st + lax.dot_general(kd_ref[slot], v16, TN_DIMS,
                                                           preferred_element_type=F32)
    o = o - jnp.mean(o, axis=-1, keepdims=True)
    o = o * lax.rsqrt(jnp.mean(o * o, axis=-1, keepdims=True) + HEAD_NORM_EPS)
    ob_ref[h] = (o * g_ref[slot]).astype(BF16)


def _mixer_kernel(alpha, x_ref, mod_ref, cs_ref, lbt_ref, lvl_ref, dmask_ref, qdec_ref, kdec_ref,
                  cdec_ref, w_ref, wpa_ref, wpb_ref, wo_ref, lng_ref, lnb_ref,
                  o_ref, u_ref, hq_ref, hlf_ref, hk_ref, hv_ref, hg_ref,
                  rq_ref, rqd_ref, rk_ref, rkd_ref, rv_ref, rg_ref,
                  oa_ref, ob_ref, sth_ref, str_ref, b_ref):
    @pl.when(pl.program_id(1) == 0)
    def _():
        sth_ref[...] = jnp.zeros_like(sth_ref)
        str_ref[...] = jnp.zeros_like(str_ref)

    x = x_ref[0]
    shift, scale, gate = mod_ref[0, 0:1, :], mod_ref[0, 1:2, :], mod_ref[0, 2:3, :]
    u_ref[...] = (x * (1.0 + scale) + shift).astype(BF16)

    assert HG_PAIRS == RET_HEADS
    hgrn_act = (hq_ref, hlf_ref, hk_ref, hv_ref, hg_ref)
    ret_act = (rq_ref, rqd_ref, rk_ref, rkd_ref, rv_ref, rg_ref)
    for i in range(HG_PAIRS):
        slot = i % 2
        _hgrn_project(i, slot, u_ref, w_ref, lbt_ref, *hgrn_act)
        _hgrn_compute(i, slot, lvl_ref, *hgrn_act, oa_ref, sth_ref, b_ref)
        _retention_project(i, slot, u_ref, w_ref, cs_ref, qdec_ref, kdec_ref, *ret_act)
        _retention_compute(i, slot, dmask_ref, cdec_ref, *ret_act, ob_ref, str_ref)
    ya = jnp.dot(oa_ref[0], wpa_ref[0], preferred_element_type=F32)
    yb = jnp.dot(ob_ref[0], wpb_ref[0], preferred_element_type=F32)
    for i in range(1, HG_PAIRS):
        ya = ya + jnp.dot(oa_ref[i], wpa_ref[i], preferred_element_type=F32)
        yb = yb + jnp.dot(ob_ref[i], wpb_ref[i], preferred_element_type=F32)
    gates = _project(u_ref[...], w_ref, BLK_GATES, 2 * D_MODEL // W_BLOCK)
    y = (jax.nn.sigmoid(gates[:, 0:D_MODEL]) * ya
         + jax.nn.sigmoid(gates[:, D_MODEL:2 * D_MODEL]) * yb)
    y = jnp.dot(y.astype(BF16), wo_ref[...], preferred_element_type=F32)
    o_ref[0] = _layer_norm(alpha * x + (1.0 + gate) * y, lng_ref[...], lnb_ref[...])


def _mixer_tables(tm):
    i = np.arange(HG_CHUNK)
    differ = i[:, None] ^ i[None, :]
    level = np.where(i[:, None] > i[None, :], np.floor(np.log2(np.maximum(differ, 1))), -1)
    log_gamma = np.log1p(-np.exp2(-5.0 - np.arange(RET_HEADS)))[:, None, None]
    r = np.arange(tm, dtype=np.float64)
    diff = r[:, None] - r[None, :]
    scale = RET_DK ** -0.5
    dmask = np.where(diff >= 0, np.exp(log_gamma * np.maximum(diff, 0.0)), 0.0) * scale
    ones = np.ones((1, 1, RET_DK // 2))
    qdec = np.exp(log_gamma * (r[None, :, None] + 1.0)) * ones
    kdec = np.exp(log_gamma * (tm - 1.0 - r[None, :, None])) * scale * ones
    cdec = np.exp(log_gamma * tm) * np.ones((1, SUBLANES, RET_DV))
    return (jnp.asarray(level, jnp.int32), jnp.asarray(dmask, F32), jnp.asarray(qdec, F32),
            jnp.asarray(kdec, F32), jnp.asarray(cdec, F32))


def _mixer(alpha, x, mod, cos_sin, lbt, w_blocks, wpa, wpb, wo, ln_g, ln_b):
    batch, seq, d = x.shape
    tm = min(MIX_TM, seq)
    tables = _mixer_tables(tm)
    tile = pl.BlockSpec((1, tm, d), lambda b, s: (b, s, 0))
    consts = (lbt,) + tables + (w_blocks, wpa, wpb, wo, ln_g, ln_b)
    return pl.pallas_call(
        functools.partial(_mixer_kernel, alpha),
        grid=(batch, seq // tm),
        in_specs=[tile,
                  pl.BlockSpec((1, 6, d), lambda b, s: (b, 0, 0)),
                  pl.BlockSpec((1, tm, RET_DK), lambda b, s: (b, s, 0))]
                 + [_resident(a.shape) for a in consts],
        out_specs=tile,
        out_shape=jax.ShapeDtypeStruct(x.shape, F32),
        scratch_shapes=[pltpu.VMEM((tm, d), BF16)]
                       + [pltpu.VMEM((2, tm, 2 * HG_D), F32)] * 5
                       + [pltpu.VMEM((2, tm, RET_DK), BF16)] * 4
                       + [pltpu.VMEM((2, tm, RET_DV), BF16),
                          pltpu.VMEM((2, tm, RET_DV), F32),
                          pltpu.VMEM((HG_PAIRS, tm, 2 * HG_D), BF16),
                          pltpu.VMEM((RET_HEADS, tm, RET_DV), BF16),
                          pltpu.VMEM((HG_HEADS, HG_D, HG_D), F32),
                          pltpu.VMEM((RET_HEADS, RET_DK, RET_DV), F32),
                          pltpu.VMEM((2 * (tm // HG_CHUNK), HG_CHUNK, HG_D), F32)],
        compiler_params=pltpu.CompilerParams(
            dimension_semantics=("arbitrary", "arbitrary"), vmem_limit_bytes=VMEM_LIMIT),
        name="mixer_sublayer",
    )(x, mod, cos_sin, *consts)


def _mlp_kernel(alpha, x_ref, mod_ref, wup_ref, bup_ref, wdn_ref, bdn_ref, lng_ref, lnb_ref,
                o_ref, h_ref):
    x = x_ref[0]
    shift, scale, gate = mod_ref[0, 3:4, :], mod_ref[0, 4:5, :], mod_ref[0, 5:6, :]
    u = (x * (1.0 + scale) + shift).astype(BF16)
    for j in range(D_FF // D_MODEL):
        cols = slice(j * D_MODEL, (j + 1) * D_MODEL)
        hj = jnp.dot(u, wup_ref[:, cols], preferred_element_type=F32) + bup_ref[:, cols]
        h_ref[:, cols] = jnp.square(jnp.maximum(hj, 0.0)).astype(BF16)
    y = jnp.dot(h_ref[...], wdn_ref[...], preferred_element_type=F32) + bdn_ref[...]
    o_ref[0] = _layer_norm(alpha * x + (1.0 + gate) * y, lng_ref[...], lnb_ref[...])


def _mlp(alpha, x, mod, wup, bup, wdn, bdn, ln_g, ln_b):
    batch, seq, d = x.shape
    tm = min(MLP_TM, seq)
    tile = pl.BlockSpec((1, tm, d), lambda b, s: (b, s, 0))
    return pl.pallas_call(
        functools.partial(_mlp_kernel, alpha),
        grid=(batch, seq // tm),
        in_specs=[tile,
                  pl.BlockSpec((1, 6, d), lambda b, s: (b, 0, 0)),
                  _resident(wup.shape), _resident(bup.shape), _resident(wdn.shape),
                  _resident(bdn.shape), _resident(ln_g.shape), _resident(ln_b.shape)],
        out_specs=tile,
        out_shape=jax.ShapeDtypeStruct(x.shape, F32),
        scratch_shapes=[pltpu.VMEM((tm, D_FF), BF16)],
        compiler_params=pltpu.CompilerParams(
            dimension_semantics=("arbitrary", "arbitrary"), vmem_limit_bytes=VMEM_LIMIT),
        name="mlp_sublayer",
    )(x, mod, wup, bup, wdn, bdn, ln_g, ln_b)


def kernel(x, c, positions, lb_logits, w_ada, b_ada, w_in, w_pa, w_pb, w_o, ln1_g, ln1_b,
           w_up, b_up, w_down, b_down, ln2_g, ln2_b):
    depth = w_in.shape[0]
    batch, seq, d = x.shape
    alpha = (2 * depth) ** 0.25
    mod = _modulation(c, w_ada, b_ada).reshape(depth, batch, 6, d)
    lbt = _lower_bound_tables(lb_logits)
    lbt = lbt.reshape(depth, 8, HG_HEADS, HG_D).transpose(0, 2, 1, 3)
    theta = (ROPE_BASE ** (-jnp.linspace(0.0, 1.0, RET_DK // 2, dtype=F32))).reshape(1, -1)
    cos_sin = _rope_table(positions, theta)
    w_blocks = _regroup_in_projection(w_in)
    for l in range(depth):
        x = _mixer(alpha, x, mod[l], cos_sin, lbt[l], w_blocks[l],
                   w_pa[l].astype(BF16).reshape(HG_PAIRS, 2 * HG_D, d),
                   w_pb[l].astype(BF16).reshape(RET_HEADS, RET_DV, d),
                   w_o[l].astype(BF16), ln1_g[l].reshape(1, d), ln1_b[l].reshape(1, d))
        x = _mlp(alpha, x, mod[l], w_up[l].astype(BF16), b_up[l].reshape(1, -1),
                 w_down[l].astype(BF16), b_down[l].reshape(1, d),
                 ln2_g[l].reshape(1, d), ln2_b[l].reshape(1, d))
    return x
```

```python
import functools

import jax
import jax.numpy as jnp
import numpy as np
from jax import lax
from jax.experimental import pallas as pl
from jax.experimental.pallas import tpu as pltpu

F32 = jnp.float32
BF16 = jnp.bfloat16

D_MODEL = 1024
HG_HEADS = 8
SUBLANES = 8
HG_D = 128
HG_PAIRS = HG_HEADS // 2
RET_HEADS = 4
RET_DK = 256
RET_DV = 512
W_BLOCK = 256
BLK_HG_Q, BLK_HG_F, BLK_HG_V, BLK_HG_G = 0, 4, 8, 12
BLK_RET_Q, BLK_RET_K, BLK_RET_V, BLK_RET_G = 16, 20, 24, 32
BLK_GATES = 40
N_BLOCKS = 48
D_FF = 4 * D_MODEL
ROPE_BASE = 10000.0
LN_EPS = 1e-5
HEAD_NORM_EPS = 1e-6
LOG2_E = 1.4426950408889634
LOCAL_ROWS = 32
LOCAL_MAX_DROP = 100.0
LEVEL_DIAGONAL, LEVEL_NONE = -2, 99

MIX_TM = 256
HG_CHUNK = 128
MLP_TM = 512
ROPE_ROWS = 1024
VMEM_LIMIT = 60 * 1024 * 1024

NT_DIMS = (((1,), (1,)), ((), ()))
TN_DIMS = (((0,), (0,)), ((), ()))


def _resident(shape):
    n = len(shape)
    return pl.BlockSpec(shape, lambda *_: (0,) * n, pipeline_mode=pl.Buffered(1))


def _layer_norm(r, g, b):
    mu = jnp.mean(r, axis=-1, keepdims=True)
    d = r - mu
    var = jnp.mean(d * d, axis=-1, keepdims=True)
    return d * lax.rsqrt(var + LN_EPS) * g + b


def _silu(z):
    return z * jax.nn.sigmoid(z)


def _mod_kernel(c_ref, w_ref, b_ref, o_ref):
    cond = _silu(c_ref[...])
    o_ref[0] = jnp.dot(cond, w_ref[0], preferred_element_type=F32,
                       precision=lax.Precision.HIGHEST) + b_ref[0]


def _modulation(c, w_ada, b_ada):
    depth, d, n = w_ada.shape
    batch = c.shape[0]
    nb = n // d
    return pl.pallas_call(
        _mod_kernel,
        grid=(depth, nb),
        in_specs=[pl.BlockSpec((batch, d), lambda l, j: (0, 0)),
                  pl.BlockSpec((1, d, d), lambda l, j: (l, 0, j)),
                  pl.BlockSpec((1, 1, d), lambda l, j: (l, 0, j))],
        out_specs=pl.BlockSpec((1, batch, d), lambda l, j: (l, 0, j)),
        out_shape=jax.ShapeDtypeStruct((depth, batch, n), F32),
        name="ada_modulation",
    )(c, w_ada, b_ada.reshape(depth, 1, n))


def _rope_kernel(pos_ref, theta_ref, o_ref):
    ang = pos_ref[0].astype(F32) * theta_ref[...]
    half = RET_DK // 2
    o_ref[0, :, 0:half] = jnp.cos(ang)
    o_ref[0, :, half:RET_DK] = jnp.sin(ang)


def _rope_table(positions, theta):
    batch, seq = positions.shape
    rows = min(ROPE_ROWS, seq)
    return pl.pallas_call(
        _rope_kernel,
        grid=(batch, seq // rows),
        in_specs=[pl.BlockSpec((1, rows, 1), lambda b, s: (b, s, 0)),
                  pl.BlockSpec(theta.shape, lambda b, s: (0, 0))],
        out_specs=pl.BlockSpec((1, rows, RET_DK), lambda b, s: (b, s, 0)),
        out_shape=jax.ShapeDtypeStruct((batch, seq, RET_DK), F32),
        name="rope_table",
    )(positions.reshape(batch, seq, 1), theta)


def _lb_kernel(depth, logit_ref, o_ref):
    z = logit_ref[...]
    e = jnp.exp(z - jnp.max(z, axis=0, keepdims=True))
    sm = e / jnp.sum(e, axis=0, keepdims=True)
    o_ref[...] = jnp.zeros_like(o_ref)
    cum = jnp.zeros_like(sm[0:1])
    for l in range(depth):
        cum = cum + sm[l:l + 1]
        lb = cum - sm[0:1]
        o_ref[l, 0:1, :] = jnp.log(lb)
        o_ref[l, 1:2, :] = jnp.log1p(-lb)
        o_ref[l, 2:3, :] = 1.0 - lb


def _lower_bound_tables(lb_logits):
    depth, n = lb_logits.shape
    return pl.pallas_call(
        functools.partial(_lb_kernel, depth),
        out_shape=jax.ShapeDtypeStruct((depth, 8, n), F32),
        name="hgrn_lower_bounds",
    )(lb_logits.astype(F32))


def _regroup_kernel(w_ref, o_ref):
    j = pl.program_id(1)
    blk = w_ref[0].astype(BF16)
    rotary = (j >= BLK_RET_Q) & (j < BLK_RET_V)

    @pl.when(rotary)
    def _():
        src = lax.broadcasted_iota(jnp.int32, (W_BLOCK, W_BLOCK), 0)
        dst = lax.broadcasted_iota(jnp.int32, (W_BLOCK, W_BLOCK), 1)
        half = W_BLOCK // 2
        want = jnp.where(dst < half, 2 * dst, 2 * (dst - half) + 1)
        perm = (src == want).astype(BF16)
        o_ref[0, 0] = jnp.dot(blk, perm, preferred_element_type=F32).astype(BF16)

    @pl.when(jnp.logical_not(rotary))
    def _():
        o_ref[0, 0] = blk


def _regroup_in_projection(w_in):
    depth, d, n = w_in.shape
    assert n == N_BLOCKS * W_BLOCK and RET_DK == W_BLOCK
    return pl.pallas_call(
        _regroup_kernel,
        grid=(depth, N_BLOCKS),
        in_specs=[pl.BlockSpec((1, d, W_BLOCK), lambda l, j: (l, 0, j))],
        out_specs=pl.BlockSpec((1, 1, d, W_BLOCK), lambda l, j: (l, j, 0, 0)),
        out_shape=jax.ShapeDtypeStruct((depth, N_BLOCKS, d, W_BLOCK), BF16),
        name="regroup_in_projection",
    )(w_in)


def _cumsum_rows(x):
    c, w = x.shape
    head_rows = lax.broadcasted_iota(jnp.int32, (SUBLANES, w), 0)
    shift = 1
    while shift < c:
        if shift < SUBLANES:
            moved = pltpu.roll(x, shift, 0)
            first = jnp.where(head_rows >= shift, moved[0:SUBLANES], 0.0)
            moved = jnp.concatenate([first, moved[SUBLANES:]], axis=0)
        else:
            moved = jnp.concatenate([jnp.zeros((shift, w), F32), x[:c - shift]], axis=0)
        x = x + moved
        shift *= 2
    return x


def _neg_abs(x):
    bits = lax.bitcast_convert_type(x, jnp.uint32) | jnp.uint32(0x80000000)
    return lax.bitcast_convert_type(bits, F32)


def _level_operands(q, k, b, b_ref, level):
    c, w = b.shape
    half = 1 << level
    if half >= SUBLANES:
        sel, expo = [], []
        for p in range(c // half):
            rows = slice(p * half, (p + 1) * half)
            mid = (p // 2) * 2 * half + half - 1
            rho = jnp.broadcast_to(b_ref[mid:mid + 1, :], (half, w))
            sel.append((q if p % 2 else k)[rows])
            expo.append(b[rows] - rho if p % 2 else rho - b[rows])
        return jnp.concatenate(sel, axis=0), jnp.exp2(jnp.concatenate(expo, axis=0))
    shape3 = (c // SUBLANES, SUBLANES, w)
    b3 = b.reshape(shape3)
    sub = lax.broadcasted_iota(jnp.int32, (1, SUBLANES, w), 1)
    rho = None
    for p in reversed(range(SUBLANES // (2 * half))):
        mid = p * 2 * half + half - 1
        piece = jnp.broadcast_to(b3[:, mid:mid + 1, :], shape3)
        rho = piece if rho is None else jnp.where(sub < (p + 1) * 2 * half, piece, rho)
    sel = jnp.where(((sub >> level) & 1) == 1, q.reshape(shape3), k.reshape(shape3))
    return sel.reshape(c, w), jnp.exp2(_neg_abs(b3 - rho)).reshape(c, w)


def _hgrn_decay_drop(b_ref):
    c = b_ref.shape[0]
    ends = [b_ref[m - 1:m, :] for m in range(LOCAL_ROWS, c + 1, LOCAL_ROWS)]
    starts = [jnp.zeros_like(ends[0])] + ends[:-1]
    return jnp.max(jnp.concatenate([s - e for s, e in zip(starts, ends)], axis=0))


def _hgrn_chunk(local_blocks, q, k, v, st_ref, lvl_ref, b_ref):
    c = q.shape[0]
    heads = (slice(0, HG_D), slice(HG_D, 2 * HG_D))
    b = b_ref[...]
    zero = jnp.zeros((c, HG_D), BF16)

    def block_diagonal(t):
        return jnp.concatenate([jnp.concatenate([t[:, heads[0]], zero], axis=1),
                                jnp.concatenate([zero, t[:, heads[1]]], axis=1)], axis=0)

    pieces = [slice(r * SUBLANES, (r + 1) * SUBLANES) for r in range(c // SUBLANES)]
    if local_blocks:
        first_level = LOCAL_ROWS.bit_length() - 1
        rho = [jnp.zeros((LOCAL_ROWS, b.shape[1]), F32)]
        rho += [jnp.broadcast_to(b_ref[m - 1:m, :], (LOCAL_ROWS, b.shape[1]))
                for m in range(LOCAL_ROWS, c, LOCAL_ROWS)]
        expo = b - jnp.concatenate(rho, axis=0)
        q_loc = (q * jnp.exp2(expo)).astype(BF16)
        k_loc = (k * jnp.exp2(-expo)).astype(BF16)
        s_local = lax.dot_general(q_loc, block_diagonal(k_loc), NT_DIMS,
                                  preferred_element_type=F32)
        scores = [jnp.where(lvl_ref[rows, :] < first_level, s_local[rows], 0.0) for rows in pieces]
    else:
        first_level = 0
        scores = [jnp.zeros((SUBLANES, 2 * c), F32)] * len(pieces)
    for level in range(first_level, c.bit_length() - 1):
        sel, dec = _level_operands(q, k, b, b_ref, level)
        z = (sel * dec).astype(BF16)
        s_level = lax.dot_general(z, block_diagonal(z), NT_DIMS, preferred_element_type=F32)
        for r, rows in enumerate(pieces):
            if rows.start >> level & 1 or (1 << level) < SUBLANES:
                scores[r] = jnp.where(lvl_ref[rows, :] == level, s_level[rows], scores[r])
    p = jnp.concatenate(scores, axis=0).astype(BF16)
    v16 = v.astype(BF16)
    o = jnp.dot(p, block_diagonal(v16), preferred_element_type=F32)
    if not local_blocks:
        qk = q * k
        diag = [jnp.broadcast_to(jnp.sum(qk[:, hd], axis=-1, keepdims=True), (c, HG_D))
                for hd in heads]
        o = o + jnp.concatenate(diag, axis=1) * v
    st = st_ref[...]
    o = o + lax.dot_general((q * jnp.exp2(b)).astype(BF16), st.astype(BF16), NT_DIMS,
                            preferred_element_type=F32)
    b_last = b_ref[c - 1:c, :]
    k_dec = (k * jnp.exp2(b_last - b)).astype(BF16)
    update = lax.dot_general(v16, k_dec, TN_DIMS, preferred_element_type=F32)
    st_dec = jnp.exp2(b_last)
    for hd in heads:
        st_ref[hd, hd] = st[hd, hd] * st_dec[:, hd] + update[hd, hd]
    return o


def _project(u, w_ref, first, count=1):
    parts = [jnp.dot(u, w_ref[first + t], preferred_element_type=F32) for t in range(count)]
    return parts[0] if count == 1 else jnp.concatenate(parts, axis=1)


def _hgrn_project(p, slot, u_ref, w_ref, lbt_ref, q_ref, k_ref, v_ref, g_ref, b_ref):
    u = u_ref[...]
    tab = jnp.concatenate([lbt_ref[2 * p], lbt_ref[2 * p + 1]], axis=1)
    log_lb, log1m_lb, one_m_lb = tab[0:1], tab[1:2], tab[2:3]
    q_ref[slot] = _silu(_project(u, w_ref, BLK_HG_Q + p))
    zf = _project(u, w_ref, BLK_HG_F + p)
    e = jnp.exp(-jnp.abs(zf))
    one_p_e = 1.0 + e
    a2 = log1m_lb + jnp.minimum(zf, 0.0) - jnp.log(one_p_e)
    lf = jnp.maximum(log_lb, a2) + jnp.log(1.0 + jnp.exp(-jnp.abs(log_lb - a2)))
    lf2 = lf * LOG2_E
    k_ref[slot] = one_m_lb * (jnp.where(zf >= 0.0, e, 1.0) / one_p_e)
    v_ref[slot] = _project(u, w_ref, BLK_HG_V + p)
    g_ref[slot] = _silu(_project(u, w_ref, BLK_HG_G + p))
    drop = None
    for ck in range(lf2.shape[0] // HG_CHUNK):
        b_ref[ck] = _cumsum_rows(lf2[ck * HG_CHUNK:(ck + 1) * HG_CHUNK])
        d = _hgrn_decay_drop(b_ref.at[ck])
        drop = d if drop is None else jnp.maximum(drop, d)
    return drop


def _hgrn_compute(local_blocks, p, slot, lvl_ref, q_ref, k_ref, v_ref, g_ref, b_ref, oa_ref,
                  st_ref):
    tm = q_ref.shape[1]
    for ck in range(tm // HG_CHUNK):
        rows = slice(ck * HG_CHUNK, (ck + 1) * HG_CHUNK)
        o = _hgrn_chunk(local_blocks, q_ref[slot, rows, :], k_ref[slot, rows, :],
                        v_ref[slot, rows, :], st_ref.at[p], lvl_ref, b_ref.at[ck])
        for j in range(2):
            cols = slice(j * HG_D, (j + 1) * HG_D)
            oj = o[:, cols]
            oj = oj * lax.rsqrt(jnp.mean(oj * oj, axis=-1, keepdims=True) + HEAD_NORM_EPS)
            oa_ref[p, rows, cols] = (oj * g_ref[slot, rows, cols]).astype(BF16)


def _retention_project(h, slot, u_ref, w_ref, cs_ref, qdec_ref, kdec_ref,
                       q_ref, qd_ref, k_ref, kd_ref, v_ref, g_ref):
    half = RET_DK // 2
    u = u_ref[...]
    cos, sin = cs_ref[0, :, 0:half], cs_ref[0, :, half:RET_DK]

    def rotated(block, dec_ref, plain_ref, scaled_ref):
        t = _project(u, w_ref, block)
        t1, t2 = t[:, 0:half], t[:, half:RET_DK]
        r1, r2 = t1 * cos - t2 * sin, t1 * sin + t2 * cos
        dec = dec_ref[h]
        plain_ref[slot] = jnp.concatenate([r1, r2], axis=1).astype(BF16)
        scaled_ref[slot] = jnp.concatenate([r1 * dec, r2 * dec], axis=1).astype(BF16)

    rotated(BLK_RET_Q + h, qdec_ref, q_ref, qd_ref)
    rotated(BLK_RET_K + h, kdec_ref, k_ref, kd_ref)
    wide = RET_DV // W_BLOCK
    v_ref[slot] = _project(u, w_ref, BLK_RET_V + wide * h, wide).astype(BF16)
    g_ref[slot] = _silu(_project(u, w_ref, BLK_RET_G + wide * h, wide))


def _retention_compute(h, slot, dmask_ref, cdec_ref, q_ref, qd_ref, k_ref, kd_ref, v_ref, g_ref,
                       ob_ref, st_ref):
    v16 = v_ref[slot]
    scores = lax.dot_general(q_ref[slot], k_ref[slot], NT_DIMS,
                             preferred_element_type=F32) * dmask_ref[h]
    st = st_ref[h]
    o = (jnp.dot(scores.astype(BF16), v16, preferred_element_type=F32)
         + jnp.dot(qd_ref[slot], st.astype(BF16), preferred_element_type=F32))
    st_ref[h] = cdec_ref[h][0:1, :] * st + lax.dot_general(kd_ref[slot], v16, TN_DIMS,
                                                           preferred_element_type=F32)
    o = o - jnp.mean(o, axis=-1, keepdims=True)
    o = o * lax.rsqrt(jnp.mean(o * o, axis=-1, keepdims=True) + HEAD_NORM_EPS)
    ob_ref[h] = (o * g_ref[slot]).astype(BF16)


def _mixer_kernel(alpha, x_ref, mod_ref, cs_ref, lbt_ref, lvl_ref, dmask_ref, qdec_ref, kdec_ref,
                  cdec_ref, w_ref, wpa_ref, wpb_ref, wo_ref, lng_ref, lnb_ref,
                  o_ref, u_ref, hq_ref, hk_ref, hv_ref, hg_ref,
                  rq_ref, rqd_ref, rk_ref, rkd_ref, rv_ref, rg_ref,
                  oa_ref, ob_ref, sth_ref, str_ref, b_ref):
    @pl.when(pl.program_id(1) == 0)
    def _():
        sth_ref[...] = jnp.zeros_like(sth_ref)
        str_ref[...] = jnp.zeros_like(str_ref)

    x = x_ref[0]
    shift, scale, gate = mod_ref[0, 0:1, :], mod_ref[0, 1:2, :], mod_ref[0, 2:3, :]
    u_ref[...] = (x * (1.0 + scale) + shift).astype(BF16)

    assert HG_PAIRS == RET_HEADS
    hgrn_act = (hq_ref, hk_ref, hv_ref, hg_ref, b_ref)
    ret_act = (rq_ref, rqd_ref, rk_ref, rkd_ref, rv_ref, rg_ref)
    for i in range(HG_PAIRS):
        slot = i % 2
        drop = _hgrn_project(i, slot, u_ref, w_ref, lbt_ref, *hgrn_act)
        compute = functools.partial(_hgrn_compute, p=i, slot=slot, lvl_ref=lvl_ref, q_ref=hq_ref,
                                    k_ref=hk_ref, v_ref=hv_ref, g_ref=hg_ref, b_ref=b_ref,
                                    oa_ref=oa_ref, st_ref=sth_ref)
        lax.cond(drop <= LOCAL_MAX_DROP, functools.partial(compute, True),
                 functools.partial(compute, False))
        _retention_project(i, slot, u_ref, w_ref, cs_ref, qdec_ref, kdec_ref, *ret_act)
        _retention_compute(i, slot, dmask_ref, cdec_ref, *ret_act, ob_ref, str_ref)
    ya = jnp.dot(oa_ref[0], wpa_ref[0], preferred_element_type=F32)
    yb = jnp.dot(ob_ref[0], wpb_ref[0], preferred_element_type=F32)
    for i in range(1, HG_PAIRS):
        ya = ya + jnp.dot(oa_ref[i], wpa_ref[i], preferred_element_type=F32)
        yb = yb + jnp.dot(ob_ref[i], wpb_ref[i], preferred_element_type=F32)
    gates = _project(u_ref[...], w_ref, BLK_GATES, 2 * D_MODEL // W_BLOCK)
    y = (jax.nn.sigmoid(gates[:, 0:D_MODEL]) * ya
         + jax.nn.sigmoid(gates[:, D_MODEL:2 * D_MODEL]) * yb)
    y = jnp.dot(y.astype(BF16), wo_ref[...], preferred_element_type=F32)
    o_ref[0] = _layer_norm(alpha * x + (1.0 + gate) * y, lng_ref[...], lnb_ref[...])


def _mixer_tables(tm):
    i = np.arange(HG_CHUNK)
    differ = i[:, None] ^ i[None, :]
    level = np.where(i[:, None] > i[None, :], np.floor(np.log2(np.maximum(differ, 1))), LEVEL_NONE)
    level[i, i] = LEVEL_DIAGONAL
    log_gamma = np.log1p(-np.exp2(-5.0 - np.arange(RET_HEADS)))[:, None, None]
    r = np.arange(tm, dtype=np.float64)
    diff = r[:, None] - r[None, :]
    scale = RET_DK ** -0.5
    dmask = np.where(diff >= 0, np.exp(log_gamma * np.maximum(diff, 0.0)), 0.0) * scale
    ones = np.ones((1, 1, RET_DK // 2))
    qdec = np.exp(log_gamma * (r[None, :, None] + 1.0)) * ones
    kdec = np.exp(log_gamma * (tm - 1.0 - r[None, :, None])) * scale * ones
    cdec = np.exp(log_gamma * tm) * np.ones((1, SUBLANES, RET_DV))
    level = np.tile(level, (1, 2))
    return (jnp.asarray(level, jnp.int32), jnp.asarray(dmask, F32), jnp.asarray(qdec, F32),
            jnp.asarray(kdec, F32), jnp.asarray(cdec, F32))


def _mixer(alpha, x, mod, cos_sin, lbt, w_blocks, wpa, wpb, wo, ln_g, ln_b):
    batch, seq, d = x.shape
    tm = min(MIX_TM, seq)
    tables = _mixer_tables(tm)
    tile = pl.BlockSpec((1, tm, d), lambda b, s: (b, s, 0))
    consts = (lbt,) + tables + (w_blocks, wpa, wpb, wo, ln_g, ln_b)
    return pl.pallas_call(
        functools.partial(_mixer_kernel, alpha),
        grid=(batch, seq // tm),
        in_specs=[tile,
                  pl.BlockSpec((1, 6, d), lambda b, s: (b, 0, 0)),
                  pl.BlockSpec((1, tm, RET_DK), lambda b, s: (b, s, 0))]
                 + [_resident(a.shape) for a in consts],
        out_specs=tile,
        out_shape=jax.ShapeDtypeStruct(x.shape, F32),
        scratch_shapes=[pltpu.VMEM((tm, d), BF16)]
                       + [pltpu.VMEM((2, tm, 2 * HG_D), F32)] * 4
                       + [pltpu.VMEM((2, tm, RET_DK), BF16)] * 4
                       + [pltpu.VMEM((2, tm, RET_DV), BF16),
                          pltpu.VMEM((2, tm, RET_DV), F32),
                          pltpu.VMEM((HG_PAIRS, tm, 2 * HG_D), BF16),
                          pltpu.VMEM((RET_HEADS, tm, RET_DV), BF16),
                          pltpu.VMEM((HG_PAIRS, 2 * HG_D, 2 * HG_D), F32),
                          pltpu.VMEM((RET_HEADS, RET_DK, RET_DV), F32),
                          pltpu.VMEM((tm // HG_CHUNK, HG_CHUNK, 2 * HG_D), F32)],
        compiler_params=pltpu.CompilerParams(
            dimension_semantics=("arbitrary", "arbitrary"), vmem_limit_bytes=VMEM_LIMIT),
        name="mixer_sublayer",
    )(x, mod, cos_sin, *consts)


def _mlp_kernel(alpha, x_ref, mod_ref, wup_ref, bup_ref, wdn_ref, bdn_ref, lng_ref, lnb_ref,
                o_ref, h_ref):
    x = x_ref[0]
    shift, scale, gate = mod_ref[0, 3:4, :], mod_ref[0, 4:5, :], mod_ref[0, 5:6, :]
    u = (x * (1.0 + scale) + shift).astype(BF16)
    for j in range(D_FF // D_MODEL):
        cols = slice(j * D_MODEL, (j + 1) * D_MODEL)
        hj = jnp.dot(u, wup_ref[:, cols], preferred_element_type=F32) + bup_ref[:, cols]
        h_ref[:, cols] = jnp.square(jnp.maximum(hj, 0.0)).astype(BF16)
    y = jnp.dot(h_ref[...], wdn_ref[...], preferred_element_type=F32) + bdn_ref[...]
    o_ref[0] = _layer_norm(alpha * x + (1.0 + gate) * y, lng_ref[...], lnb_ref[...])


def _mlp(alpha, x, mod, wup, bup, wdn, bdn, ln_g, ln_b):
    batch, seq, d = x.shape
    tm = min(MLP_TM, seq)
    tile = pl.BlockSpec((1, tm, d), lambda b, s: (b, s, 0))
    return pl.pallas_call(
        functools.partial(_mlp_kernel, alpha),
        grid=(batch, seq // tm),
        in_specs=[tile,
                  pl.BlockSpec((1, 6, d), lambda b, s: (b, 0, 0)),
                  _resident(wup.shape), _resident(bup.shape), _resident(wdn.shape),
                  _resident(bdn.shape), _resident(ln_g.shape), _resident(ln_b.shape)],
        out_specs=tile,
        out_shape=jax.ShapeDtypeStruct(x.shape, F32),
        scratch_shapes=[pltpu.VMEM((tm, D_FF), BF16)],
        compiler_params=pltpu.CompilerParams(
            dimension_semantics=("arbitrary", "arbitrary"), vmem_limit_bytes=VMEM_LIMIT),
        name="mlp_sublayer",
    )(x, mod, wup, bup, wdn, bdn, ln_g, ln_b)


def kernel(x, c, positions, lb_logits, w_ada, b_ada, w_in, w_pa, w_pb, w_o, ln1_g, ln1_b,
           w_up, b_up, w_down, b_down, ln2_g, ln2_b):
    depth = w_in.shape[0]
    batch, seq, d = x.shape
    alpha = (2 * depth) ** 0.25
    mod = _modulation(c, w_ada, b_ada).reshape(depth, batch, 6, d)
    lbt = _lower_bound_tables(lb_logits)
    lbt = lbt.reshape(depth, 8, HG_HEADS, HG_D).transpose(0, 2, 1, 3)
    theta = (ROPE_BASE ** (-jnp.linspace(0.0, 1.0, RET_DK // 2, dtype=F32))).reshape(1, -1)
    cos_sin = _rope_table(positions, theta)
    w_blocks = _regroup_in_projection(w_in)
    for l in range(depth):
        x = _mixer(alpha, x, mod[l], cos_sin, lbt[l], w_blocks[l],
                   w_pa[l].astype(BF16).reshape(HG_PAIRS, 2 * HG_D, d),
                   w_pb[l].astype(BF16).reshape(RET_HEADS, RET_DV, d),
                   w_o[l].astype(BF16), ln1_g[l].reshape(1, d), ln1_b[l].reshape(1, d))
        x = _mlp(alpha, x, mod[l], w_up[l].astype(BF16), b_up[l].reshape(1, -1),
                 w_down[l].astype(BF16), b_down[l].reshape(1, d),
                 ln2_g[l].reshape(1, d), ln2_b[l].reshape(1, d))
    return x
```

```python
import functools

import jax
import jax.numpy as jnp
import numpy as np
from jax import lax
from jax.experimental import pallas as pl
from jax.experimental.pallas import tpu as pltpu

F32 = jnp.float32
BF16 = jnp.bfloat16

D_MODEL = 1024
HG_HEADS = 8
SUBLANES = 8
HG_D = 128
HG_PAIRS = HG_HEADS // 2
RET_HEADS = 4
RET_DK = 256
RET_DV = 512
W_BLOCK = 256
BLK_HG_Q, BLK_HG_F, BLK_HG_V, BLK_HG_G = 0, 4, 8, 12
BLK_RET_Q, BLK_RET_K, BLK_RET_V, BLK_RET_G = 16, 20, 24, 32
BLK_GATES = 40
N_BLOCKS = 48
D_FF = 4 * D_MODEL
ROPE_BASE = 10000.0
LN_EPS = 1e-5
HEAD_NORM_EPS = 1e-6
LOG2_E = 1.4426950408889634
LOCAL_ROWS = 32
LOCAL_MAX_DROP = 100.0
LEVEL_DIAGONAL, LEVEL_NONE = -2, 99

MIX_TM = 256
HG_CHUNK = 128
MLP_TM = 512
ROPE_ROWS = 1024
VMEM_LIMIT = 60 * 1024 * 1024

NT_DIMS = (((1,), (1,)), ((), ()))
TN_DIMS = (((0,), (0,)), ((), ()))


def _resident(shape):
    n = len(shape)
    return pl.BlockSpec(shape, lambda *_: (0,) * n, pipeline_mode=pl.Buffered(1))


def _layer_norm(r, g, b):
    mu = jnp.mean(r, axis=-1, keepdims=True)
    d = r - mu
    var = jnp.mean(d * d, axis=-1, keepdims=True)
    return d * lax.rsqrt(var + LN_EPS) * g + b


def _silu(z):
    return z * jax.nn.sigmoid(z)


def _mod_kernel(c_ref, w_ref, b_ref, o_ref):
    cond = _silu(c_ref[...])
    o_ref[0] = jnp.dot(cond, w_ref[0], preferred_element_type=F32,
                       precision=lax.Precision.HIGHEST) + b_ref[0]


def _modulation(c, w_ada, b_ada):
    depth, d, n = w_ada.shape
    batch = c.shape[0]
    nb = n // d
    return pl.pallas_call(
        _mod_kernel,
        grid=(depth, nb),
        in_specs=[pl.BlockSpec((batch, d), lambda l, j: (0, 0)),
                  pl.BlockSpec((1, d, d), lambda l, j: (l, 0, j)),
                  pl.BlockSpec((1, 1, d), lambda l, j: (l, 0, j))],
        out_specs=pl.BlockSpec((1, batch, d), lambda l, j: (l, 0, j)),
        out_shape=jax.ShapeDtypeStruct((depth, batch, n), F32),
        name="ada_modulation",
    )(c, w_ada, b_ada.reshape(depth, 1, n))


def _rope_kernel(pos_ref, theta_ref, o_ref):
    ang = pos_ref[0].astype(F32) * theta_ref[...]
    half = RET_DK // 2
    o_ref[0, :, 0:half] = jnp.cos(ang)
    o_ref[0, :, half:RET_DK] = jnp.sin(ang)


def _rope_table(positions, theta):
    batch, seq = positions.shape
    rows = min(ROPE_ROWS, seq)
    return pl.pallas_call(
        _rope_kernel,
        grid=(batch, seq // rows),
        in_specs=[pl.BlockSpec((1, rows, 1), lambda b, s: (b, s, 0)),
                  pl.BlockSpec(theta.shape, lambda b, s: (0, 0))],
        out_specs=pl.BlockSpec((1, rows, RET_DK), lambda b, s: (b, s, 0)),
        out_shape=jax.ShapeDtypeStruct((batch, seq, RET_DK), F32),
        name="rope_table",
    )(positions.reshape(batch, seq, 1), theta)


def _lb_kernel(depth, logit_ref, o_ref):
    z = logit_ref[...]
    e = jnp.exp(z - jnp.max(z, axis=0, keepdims=True))
    sm = e / jnp.sum(e, axis=0, keepdims=True)
    o_ref[...] = jnp.zeros_like(o_ref)
    cum = jnp.zeros_like(sm[0:1])
    for l in range(depth):
        cum = cum + sm[l:l + 1]
        lb = cum - sm[0:1]
        o_ref[l, 0:1, :] = jnp.log(lb)
        o_ref[l, 1:2, :] = jnp.log1p(-lb)
        o_ref[l, 2:3, :] = 1.0 - lb


def _lower_bound_tables(lb_logits):
    depth, n = lb_logits.shape
    return pl.pallas_call(
        functools.partial(_lb_kernel, depth),
        out_shape=jax.ShapeDtypeStruct((depth, 8, n), F32),
        name="hgrn_lower_bounds",
    )(lb_logits.astype(F32))


def _regroup_kernel(w_ref, o_ref):
    j = pl.program_id(1)
    blk = w_ref[0].astype(BF16)
    rotary = (j >= BLK_RET_Q) & (j < BLK_RET_V)

    @pl.when(rotary)
    def _():
        src = lax.broadcasted_iota(jnp.int32, (W_BLOCK, W_BLOCK), 0)
        dst = lax.broadcasted_iota(jnp.int32, (W_BLOCK, W_BLOCK), 1)
        half = W_BLOCK // 2
        want = jnp.where(dst < half, 2 * dst, 2 * (dst - half) + 1)
        perm = (src == want).astype(BF16)
        o_ref[0, 0] = jnp.dot(blk, perm, preferred_element_type=F32).astype(BF16)

    @pl.when(jnp.logical_not(rotary))
    def _():
        o_ref[0, 0] = blk


def _regroup_in_projection(w_in):
    depth, d, n = w_in.shape
    assert n == N_BLOCKS * W_BLOCK and RET_DK == W_BLOCK
    return pl.pallas_call(
        _regroup_kernel,
        grid=(depth, N_BLOCKS),
        in_specs=[pl.BlockSpec((1, d, W_BLOCK), lambda l, j: (l, 0, j))],
        out_specs=pl.BlockSpec((1, 1, d, W_BLOCK), lambda l, j: (l, j, 0, 0)),
        out_shape=jax.ShapeDtypeStruct((depth, N_BLOCKS, d, W_BLOCK), BF16),
        name="regroup_in_projection",
    )(w_in)


def _cumsum_rows(x):
    c, w = x.shape
    head_rows = lax.broadcasted_iota(jnp.int32, (SUBLANES, w), 0)
    shift = 1
    while shift < c:
        if shift < SUBLANES:
            moved = pltpu.roll(x, shift, 0)
            first = jnp.where(head_rows >= shift, moved[0:SUBLANES], 0.0)
            moved = jnp.concatenate([first, moved[SUBLANES:]], axis=0)
        else:
            moved = jnp.concatenate([jnp.zeros((shift, w), F32), x[:c - shift]], axis=0)
        x = x + moved
        shift *= 2
    return x


def _neg_abs(x):
    bits = lax.bitcast_convert_type(x, jnp.uint32) | jnp.uint32(0x80000000)
    return lax.bitcast_convert_type(bits, F32)


def _level_operands(q, k, b, b_ref, level):
    c, w = b.shape
    half = 1 << level
    if half >= SUBLANES:
        sel, expo = [], []
        for p in range(c // half):
            rows = slice(p * half, (p + 1) * half)
            mid = (p // 2) * 2 * half + half - 1
            rho = jnp.broadcast_to(b_ref[mid:mid + 1, :], (half, w))
            sel.append((q if p % 2 else k)[rows])
            expo.append(b[rows] - rho if p % 2 else rho - b[rows])
        return jnp.concatenate(sel, axis=0), jnp.exp2(jnp.concatenate(expo, axis=0))
    shape3 = (c // SUBLANES, SUBLANES, w)
    b3 = b.reshape(shape3)
    sub = lax.broadcasted_iota(jnp.int32, (1, SUBLANES, w), 1)
    rho = None
    for p in reversed(range(SUBLANES // (2 * half))):
        mid = p * 2 * half + half - 1
        piece = jnp.broadcast_to(b3[:, mid:mid + 1, :], shape3)
        rho = piece if rho is None else jnp.where(sub < (p + 1) * 2 * half, piece, rho)
    sel = jnp.where(((sub >> level) & 1) == 1, q.reshape(shape3), k.reshape(shape3))
    return sel.reshape(c, w), jnp.exp2(_neg_abs(b3 - rho)).reshape(c, w)


def _hgrn_decay_drop(b_ref):
    c = b_ref.shape[0]
    ends = [b_ref[m - 1:m, :] for m in range(LOCAL_ROWS, c + 1, LOCAL_ROWS)]
    starts = [jnp.zeros_like(ends[0])] + ends[:-1]
    return jnp.max(jnp.concatenate([s - e for s, e in zip(starts, ends)], axis=0))


def _hgrn_chunk(local_blocks, q, k, v, st_ref, lvl_ref, b_ref):
    c = q.shape[0]
    heads = (slice(0, HG_D), slice(HG_D, 2 * HG_D))
    b = b_ref[...]
    zero = jnp.zeros((c, HG_D), BF16)

    def block_diagonal(t):
        return jnp.concatenate([jnp.concatenate([t[:, heads[0]], zero], axis=1),
                                jnp.concatenate([zero, t[:, heads[1]]], axis=1)], axis=0)

    pieces = [slice(r * SUBLANES, (r + 1) * SUBLANES) for r in range(c // SUBLANES)]
    if local_blocks:
        first_level = LOCAL_ROWS.bit_length() - 1
        rho = [jnp.zeros((LOCAL_ROWS, b.shape[1]), F32)]
        rho += [jnp.broadcast_to(b_ref[m - 1:m, :], (LOCAL_ROWS, b.shape[1]))
                for m in range(LOCAL_ROWS, c, LOCAL_ROWS)]
        expo = b - jnp.concatenate(rho, axis=0)
        q_loc = (q * jnp.exp2(expo)).astype(BF16)
        k_loc = (k * jnp.exp2(-expo)).astype(BF16)
        s_local = lax.dot_general(q_loc, block_diagonal(k_loc), NT_DIMS,
                                  preferred_element_type=F32)
        scores = [jnp.where(lvl_ref[rows, :] < first_level, s_local[rows], 0.0) for rows in pieces]
    else:
        first_level = 0
        scores = [jnp.zeros((SUBLANES, 2 * c), F32)] * len(pieces)
    for level in range(first_level, c.bit_length() - 1):
        sel, dec = _level_operands(q, k, b, b_ref, level)
        z = (sel * dec).astype(BF16)
        s_level = lax.dot_general(z, block_diagonal(z), NT_DIMS, preferred_element_type=F32)
        for r, rows in enumerate(pieces):
            if rows.start >> level & 1 or (1 << level) < SUBLANES:
                scores[r] = jnp.where(lvl_ref[rows, :] == level, s_level[rows], scores[r])
    p = jnp.concatenate(scores, axis=0).astype(BF16)
    v16 = v.astype(BF16)
    o = jnp.dot(p, block_diagonal(v16), preferred_element_type=F32)
    if not local_blocks:
        qk = q * k
        diag = [jnp.broadcast_to(jnp.sum(qk[:, hd], axis=-1, keepdims=True), (c, HG_D))
                for hd in heads]
        o = o + jnp.concatenate(diag, axis=1) * v
    st = st_ref[...]
    o = o + lax.dot_general((q * jnp.exp2(b)).astype(BF16), st.astype(BF16), NT_DIMS,
                            preferred_element_type=F32)
    b_last = b_ref[c - 1:c, :]
    k_dec = (k * jnp.exp2(b_last - b)).astype(BF16)
    update = lax.dot_general(v16, k_dec, TN_DIMS, preferred_element_type=F32)
    st_dec = jnp.exp2(b_last)
    for hd in heads:
        st_ref[hd, hd] = st[hd, hd] * st_dec[:, hd] + update[hd, hd]
    return o


def _project(u, w_ref, first, count=1):
    parts = [jnp.dot(u, w_ref[first + t], preferred_element_type=F32) for t in range(count)]
    return parts[0] if count == 1 else jnp.concatenate(parts, axis=1)


def _hgrn_project(p, slot, u_ref, w_ref, lbt_ref, q_ref, k_ref, v_ref, g_ref, b_ref):
    u = u_ref[...]
    tab = jnp.concatenate([lbt_ref[2 * p], lbt_ref[2 * p + 1]], axis=1)
    log_lb, log1m_lb, one_m_lb = tab[0:1], tab[1:2], tab[2:3]
    q_ref[slot] = _silu(_project(u, w_ref, BLK_HG_Q + p))
    zf = _project(u, w_ref, BLK_HG_F + p)
    e = jnp.exp(-jnp.abs(zf))
    one_p_e = 1.0 + e
    a2 = log1m_lb + jnp.minimum(zf, 0.0) - jnp.log(one_p_e)
    lf = jnp.maximum(log_lb, a2) + jnp.log(1.0 + jnp.exp(-jnp.abs(log_lb - a2)))
    lf2 = lf * LOG2_E
    k_ref[slot] = one_m_lb * (jnp.where(zf >= 0.0, e, 1.0) / one_p_e)
    v_ref[slot] = _project(u, w_ref, BLK_HG_V + p)
    g_ref[slot] = _silu(_project(u, w_ref, BLK_HG_G + p))
    drop = None
    for ck in range(lf2.shape[0] // HG_CHUNK):
        b_ref[slot, ck] = _cumsum_rows(lf2[ck * HG_CHUNK:(ck + 1) * HG_CHUNK])
        d = _hgrn_decay_drop(b_ref.at[slot, ck])
        drop = d if drop is None else jnp.maximum(drop, d)
    return drop


def _hgrn_compute(local_blocks, p, slot, lvl_ref, q_ref, k_ref, v_ref, g_ref, b_ref, oa_ref,
                  st_ref):
    tm = q_ref.shape[1]
    for ck in range(tm // HG_CHUNK):
        rows = slice(ck * HG_CHUNK, (ck + 1) * HG_CHUNK)
        o = _hgrn_chunk(local_blocks, q_ref[slot, rows, :], k_ref[slot, rows, :],
                        v_ref[slot, rows, :], st_ref.at[p], lvl_ref, b_ref.at[slot, ck])
        for j in range(2):
            cols = slice(j * HG_D, (j + 1) * HG_D)
            oj = o[:, cols]
            oj = oj * lax.rsqrt(jnp.mean(oj * oj, axis=-1, keepdims=True) + HEAD_NORM_EPS)
            oa_ref[p, rows, cols] = (oj * g_ref[slot, rows, cols]).astype(BF16)


def _retention_project(h, slot, u_ref, w_ref, cs_ref, qdec_ref, kdec_ref,
                       q_ref, qd_ref, k_ref, kd_ref, v_ref, g_ref):
    half = RET_DK // 2
    u = u_ref[...]
    cos, sin = cs_ref[0, :, 0:half], cs_ref[0, :, half:RET_DK]

    def rotated(block, dec_ref, plain_ref, scaled_ref):
        t = _project(u, w_ref, block)
        t1, t2 = t[:, 0:half], t[:, half:RET_DK]
        r1, r2 = t1 * cos - t2 * sin, t1 * sin + t2 * cos
        dec = dec_ref[h]
        plain_ref[slot] = jnp.concatenate([r1, r2], axis=1).astype(BF16)
        scaled_ref[slot] = jnp.concatenate([r1 * dec, r2 * dec], axis=1).astype(BF16)

    rotated(BLK_RET_Q + h, qdec_ref, q_ref, qd_ref)
    rotated(BLK_RET_K + h, kdec_ref, k_ref, kd_ref)
    wide = RET_DV // W_BLOCK
    v_ref[slot] = _project(u, w_ref, BLK_RET_V + wide * h, wide).astype(BF16)
    g_ref[slot] = _silu(_project(u, w_ref, BLK_RET_G + wide * h, wide))


def _retention_compute(h, slot, dmask_ref, cdec_ref, q_ref, qd_ref, k_ref, kd_ref, v_ref, g_ref,
                       ob_ref, st_ref):
    v16 = v_ref[slot]
    scores = lax.dot_general(q_ref[slot], k_ref[slot], NT_DIMS,
                             preferred_element_type=F32) * dmask_ref[h]
    st = st_ref[h]
    o = (jnp.dot(scores.astype(BF16), v16, preferred_element_type=F32)
         + jnp.dot(qd_ref[slot], st.astype(BF16), preferred_element_type=F32))
    st_ref[h] = cdec_ref[h][0:1, :] * st + lax.dot_general(kd_ref[slot], v16, TN_DIMS,
                                                           preferred_element_type=F32)
    o = o - jnp.mean(o, axis=-1, keepdims=True)
    o = o * lax.rsqrt(jnp.mean(o * o, axis=-1, keepdims=True) + HEAD_NORM_EPS)
    ob_ref[h] = (o * g_ref[slot]).astype(BF16)


def _mixer_kernel(alpha, x_ref, mod_ref, cs_ref, lbt_ref, lvl_ref, dmask_ref, qdec_ref, kdec_ref,
                  cdec_ref, w_ref, wpa_ref, wpb_ref, wo_ref, lng_ref, lnb_ref,
                  o_ref, u_ref, hq_ref, hk_ref, hv_ref, hg_ref,
                  rq_ref, rqd_ref, rk_ref, rkd_ref, rv_ref, rg_ref,
                  oa_ref, ob_ref, sth_ref, str_ref, b_ref):
    @pl.when(pl.program_id(1) == 0)
    def _():
        sth_ref[...] = jnp.zeros_like(sth_ref)
        str_ref[...] = jnp.zeros_like(str_ref)

    x = x_ref[0]
    shift, scale, gate = mod_ref[0, 0:1, :], mod_ref[0, 1:2, :], mod_ref[0, 2:3, :]
    u_ref[...] = (x * (1.0 + scale) + shift).astype(BF16)

    assert HG_PAIRS == RET_HEADS
    hgrn_act = (hq_ref, hk_ref, hv_ref, hg_ref, b_ref)
    ret_act = (rq_ref, rqd_ref, rk_ref, rkd_ref, rv_ref, rg_ref)
    drop = None
    for i in range(HG_PAIRS):
        d = _hgrn_project(i, i, u_ref, w_ref, lbt_ref, *hgrn_act)
        drop = d if drop is None else jnp.maximum(drop, d)

    def rest(local_blocks):
        for i in range(HG_PAIRS):
            _hgrn_compute(local_blocks, i, i, lvl_ref, *hgrn_act, oa_ref, sth_ref)
            _retention_project(i, i % 2, u_ref, w_ref, cs_ref, qdec_ref, kdec_ref, *ret_act)
            _retention_compute(i, i % 2, dmask_ref, cdec_ref, *ret_act, ob_ref, str_ref)
        ya = jnp.dot(oa_ref[0], wpa_ref[0], preferred_element_type=F32)
        yb = jnp.dot(ob_ref[0], wpb_ref[0], preferred_element_type=F32)
        for i in range(1, HG_PAIRS):
            ya = ya + jnp.dot(oa_ref[i], wpa_ref[i], preferred_element_type=F32)
            yb = yb + jnp.dot(ob_ref[i], wpb_ref[i], preferred_element_type=F32)
        gates = _project(u_ref[...], w_ref, BLK_GATES, 2 * D_MODEL // W_BLOCK)
        y = (jax.nn.sigmoid(gates[:, 0:D_MODEL]) * ya
             + jax.nn.sigmoid(gates[:, D_MODEL:2 * D_MODEL]) * yb)
        y = jnp.dot(y.astype(BF16), wo_ref[...], preferred_element_type=F32)
        o_ref[0] = _layer_norm(alpha * x_ref[0] + (1.0 + gate) * y, lng_ref[...], lnb_ref[...])

    lax.cond(drop <= LOCAL_MAX_DROP, functools.partial(rest, True), functools.partial(rest, False))


def _mixer_tables(tm):
    i = np.arange(HG_CHUNK)
    differ = i[:, None] ^ i[None, :]
    level = np.where(i[:, None] > i[None, :], np.floor(np.log2(np.maximum(differ, 1))), LEVEL_NONE)
    level[i, i] = LEVEL_DIAGONAL
    log_gamma = np.log1p(-np.exp2(-5.0 - np.arange(RET_HEADS)))[:, None, None]
    r = np.arange(tm, dtype=np.float64)
    diff = r[:, None] - r[None, :]
    scale = RET_DK ** -0.5
    dmask = np.where(diff >= 0, np.exp(log_gamma * np.maximum(diff, 0.0)), 0.0) * scale
    ones = np.ones((1, 1, RET_DK // 2))
    qdec = np.exp(log_gamma * (r[None, :, None] + 1.0)) * ones
    kdec = np.exp(log_gamma * (tm - 1.0 - r[None, :, None])) * scale * ones
    cdec = np.exp(log_gamma * tm) * np.ones((1, SUBLANES, RET_DV))
    level = np.tile(level, (1, 2))
    return (jnp.asarray(level, jnp.int32), jnp.asarray(dmask, F32), jnp.asarray(qdec, F32),
            jnp.asarray(kdec, F32), jnp.asarray(cdec, F32))


def _mixer(alpha, x, mod, cos_sin, lbt, w_blocks, wpa, wpb, wo, ln_g, ln_b):
    batch, seq, d = x.shape
    tm = min(MIX_TM, seq)
    tables = _mixer_tables(tm)
    tile = pl.BlockSpec((1, tm, d), lambda b, s: (b, s, 0))
    consts = (lbt,) + tables + (w_blocks, wpa, wpb, wo, ln_g, ln_b)
    return pl.pallas_call(
        functools.partial(_mixer_kernel, alpha),
        grid=(batch, seq // tm),
        in_specs=[tile,
                  pl.BlockSpec((1, 6, d), lambda b, s: (b, 0, 0)),
                  pl.BlockSpec((1, tm, RET_DK), lambda b, s: (b, s, 0))]
                 + [_resident(a.shape) for a in consts],
        out_specs=tile,
        out_shape=jax.ShapeDtypeStruct(x.shape, F32),
        scratch_shapes=[pltpu.VMEM((tm, d), BF16)]
                       + [pltpu.VMEM((HG_PAIRS, tm, 2 * HG_D), F32)] * 4
                       + [pltpu.VMEM((2, tm, RET_DK), BF16)] * 4
                       + [pltpu.VMEM((2, tm, RET_DV), BF16),
                          pltpu.VMEM((2, tm, RET_DV), F32),
                          pltpu.VMEM((HG_PAIRS, tm, 2 * HG_D), BF16),
                          pltpu.VMEM((RET_HEADS, tm, RET_DV), BF16),
                          pltpu.VMEM((HG_PAIRS, 2 * HG_D, 2 * HG_D), F32),
                          pltpu.VMEM((RET_HEADS, RET_DK, RET_DV), F32),
                          pltpu.VMEM((HG_PAIRS, tm // HG_CHUNK, HG_CHUNK, 2 * HG_D), F32)],
        compiler_params=pltpu.CompilerParams(
            dimension_semantics=("arbitrary", "arbitrary"), vmem_limit_bytes=VMEM_LIMIT),
        name="mixer_sublayer",
    )(x, mod, cos_sin, *consts)


def _mlp_kernel(alpha, x_ref, mod_ref, wup_ref, bup_ref, wdn_ref, bdn_ref, lng_ref, lnb_ref,
                o_ref, h_ref):
    x = x_ref[0]
    shift, scale, gate = mod_ref[0, 3:4, :], mod_ref[0, 4:5, :], mod_ref[0, 5:6, :]
    u = (x * (1.0 + scale) + shift).astype(BF16)
    for j in range(D_FF // D_MODEL):
        cols = slice(j * D_MODEL, (j + 1) * D_MODEL)
        hj = jnp.dot(u, wup_ref[:, cols], preferred_element_type=F32) + bup_ref[:, cols]
        h_ref[:, cols] = jnp.square(jnp.maximum(hj, 0.0)).astype(BF16)
    y = jnp.dot(h_ref[...], wdn_ref[...], preferred_element_type=F32) + bdn_ref[...]
    o_ref[0] = _layer_norm(alpha * x + (1.0 + gate) * y, lng_ref[...], lnb_ref[...])


def _mlp(alpha, x, mod, wup, bup, wdn, bdn, ln_g, ln_b):
    batch, seq, d = x.shape
    tm = min(MLP_TM, seq)
    tile = pl.BlockSpec((1, tm, d), lambda b, s: (b, s, 0))
    return pl.pallas_call(
        functools.partial(_mlp_kernel, alpha),
        grid=(batch, seq // tm),
        in_specs=[tile,
                  pl.BlockSpec((1, 6, d), lambda b, s: (b, 0, 0)),
                  _resident(wup.shape), _resident(bup.shape), _resident(wdn.shape),
                  _resident(bdn.shape), _resident(ln_g.shape), _resident(ln_b.shape)],
        out_specs=tile,
        out_shape=jax.ShapeDtypeStruct(x.shape, F32),
        scratch_shapes=[pltpu.VMEM((tm, D_FF), BF16)],
        compiler_params=pltpu.CompilerParams(
            dimension_semantics=("arbitrary", "arbitrary"), vmem_limit_bytes=VMEM_LIMIT),
        name="mlp_sublayer",
    )(x, mod, wup, bup, wdn, bdn, ln_g, ln_b)


def kernel(x, c, positions, lb_logits, w_ada, b_ada, w_in, w_pa, w_pb, w_o, ln1_g, ln1_b,
           w_up, b_up, w_down, b_down, ln2_g, ln2_b):
    depth = w_in.shape[0]
    batch, seq, d = x.shape
    alpha = (2 * depth) ** 0.25
    mod = _modulation(c, w_ada, b_ada).reshape(depth, batch, 6, d)
    lbt = _lower_bound_tables(lb_logits)
    lbt = lbt.reshape(depth, 8, HG_HEADS, HG_D).transpose(0, 2, 1, 3)
    theta = (ROPE_BASE ** (-jnp.linspace(0.0, 1.0, RET_DK // 2, dtype=F32))).reshape(1, -1)
    cos_sin = _rope_table(positions, theta)
    w_blocks = _regroup_in_projection(w_in)
    for l in range(depth):
        x = _mixer(alpha, x, mod[l], cos_sin, lbt[l], w_blocks[l],
                   w_pa[l].astype(BF16).reshape(HG_PAIRS, 2 * HG_D, d),
                   w_pb[l].astype(BF16).reshape(RET_HEADS, RET_DV, d),
                   w_o[l].astype(BF16), ln1_g[l].reshape(1, d), ln1_b[l].reshape(1, d))
        x = _mlp(alpha, x, mod[l], w_up[l].astype(BF16), b_up[l].reshape(1, -1),
                 w_down[l].astype(BF16), b_down[l].reshape(1, d),
                 ln2_g[l].reshape(1, d), ln2_b[l].reshape(1, d))
    return x
```

```python
import functools

import jax
import jax.numpy as jnp
import numpy as np
from jax import lax
from jax.experimental import pallas as pl
from jax.experimental.pallas import tpu as pltpu

F32 = jnp.float32
BF16 = jnp.bfloat16

D_MODEL = 1024
HG_HEADS = 8
SUBLANES = 8
HG_D = 128
HG_PAIRS = HG_HEADS // 2
RET_HEADS = 4
RET_DK = 256
RET_DV = 512
W_BLOCK = 256
BLK_HG_Q, BLK_HG_F, BLK_HG_V, BLK_HG_G = 0, 4, 8, 12
BLK_RET_Q, BLK_RET_K, BLK_RET_V, BLK_RET_G = 16, 20, 24, 32
BLK_GATES = 40
N_BLOCKS = 48
REGROUP_BLOCKS = 4
D_FF = 4 * D_MODEL
ROPE_BASE = 10000.0
LN_EPS = 1e-5
HEAD_NORM_EPS = 1e-6
LOG2_E = 1.4426950408889634
LOCAL_ROWS = 32
LOCAL_MAX_DROP = 100.0
LEVEL_DIAGONAL, LEVEL_NONE = -2, 99

MIX_TM = 256
HG_CHUNK = 128
MLP_TM = 512
ROPE_ROWS = 1024
VMEM_LIMIT = 60 * 1024 * 1024

NT_DIMS = (((1,), (1,)), ((), ()))
TN_DIMS = (((0,), (0,)), ((), ()))


def _resident(shape):
    n = len(shape)
    return pl.BlockSpec(shape, lambda *_: (0,) * n, pipeline_mode=pl.Buffered(1))


def _layer_norm(r, g, b):
    mu = jnp.mean(r, axis=-1, keepdims=True)
    d = r - mu
    var = jnp.mean(d * d, axis=-1, keepdims=True)
    return d * lax.rsqrt(var + LN_EPS) * g + b


def _silu(z):
    return z * jax.nn.sigmoid(z)


def _mod_kernel(c_ref, w_ref, b_ref, o_ref):
    cond = _silu(c_ref[...])
    o_ref[0] = jnp.dot(cond, w_ref[0], preferred_element_type=F32,
                       precision=lax.Precision.HIGHEST) + b_ref[0]


def _modulation(c, w_ada, b_ada):
    depth, d, n = w_ada.shape
    batch = c.shape[0]
    nb = n // d
    return pl.pallas_call(
        _mod_kernel,
        grid=(depth, nb),
        in_specs=[pl.BlockSpec((batch, d), lambda l, j: (0, 0)),
                  pl.BlockSpec((1, d, d), lambda l, j: (l, 0, j)),
                  pl.BlockSpec((1, 1, d), lambda l, j: (l, 0, j))],
        out_specs=pl.BlockSpec((1, batch, d), lambda l, j: (l, 0, j)),
        out_shape=jax.ShapeDtypeStruct((depth, batch, n), F32),
        name="ada_modulation",
    )(c, w_ada, b_ada.reshape(depth, 1, n))


def _rope_kernel(pos_ref, theta_ref, o_ref):
    ang = pos_ref[0].astype(F32) * theta_ref[...]
    half = RET_DK // 2
    o_ref[0, :, 0:half] = jnp.cos(ang)
    o_ref[0, :, half:RET_DK] = jnp.sin(ang)


def _rope_table(positions, theta):
    batch, seq = positions.shape
    rows = min(ROPE_ROWS, seq)
    return pl.pallas_call(
        _rope_kernel,
        grid=(batch, seq // rows),
        in_specs=[pl.BlockSpec((1, rows, 1), lambda b, s: (b, s, 0)),
                  pl.BlockSpec(theta.shape, lambda b, s: (0, 0))],
        out_specs=pl.BlockSpec((1, rows, RET_DK), lambda b, s: (b, s, 0)),
        out_shape=jax.ShapeDtypeStruct((batch, seq, RET_DK), F32),
        name="rope_table",
    )(positions.reshape(batch, seq, 1), theta)


def _lb_kernel(depth, logit_ref, o_ref):
    z = logit_ref[...]
    e = jnp.exp(z - jnp.max(z, axis=0, keepdims=True))
    sm = e / jnp.sum(e, axis=0, keepdims=True)
    o_ref[...] = jnp.zeros_like(o_ref)
    cum = jnp.zeros_like(sm[0:1])
    for l in range(depth):
        cum = cum + sm[l:l + 1]
        lb = cum - sm[0:1]
        o_ref[l, 0:1, :] = jnp.log(lb)
        o_ref[l, 1:2, :] = jnp.log1p(-lb)
        o_ref[l, 2:3, :] = 1.0 - lb


def _lower_bound_tables(lb_logits):
    depth, n = lb_logits.shape
    return pl.pallas_call(
        functools.partial(_lb_kernel, depth),
        out_shape=jax.ShapeDtypeStruct((depth, 8, n), F32),
        name="hgrn_lower_bounds",
    )(lb_logits.astype(F32))


def _regroup_kernel(w_ref, o_ref):
    j = pl.program_id(1) * REGROUP_BLOCKS
    rotary = (j >= BLK_RET_Q) & (j < BLK_RET_V)

    @pl.when(rotary)
    def _():
        src = lax.broadcasted_iota(jnp.int32, (W_BLOCK, W_BLOCK), 0)
        dst = lax.broadcasted_iota(jnp.int32, (W_BLOCK, W_BLOCK), 1)
        half = W_BLOCK // 2
        want = jnp.where(dst < half, 2 * dst, 2 * (dst - half) + 1)
        perm = (src == want).astype(BF16)
        for t in range(REGROUP_BLOCKS):
            blk = w_ref[0, :, t * W_BLOCK:(t + 1) * W_BLOCK].astype(BF16)
            o_ref[0, t] = jnp.dot(blk, perm, preferred_element_type=F32).astype(BF16)

    @pl.when(jnp.logical_not(rotary))
    def _():
        for t in range(REGROUP_BLOCKS):
            o_ref[0, t] = w_ref[0, :, t * W_BLOCK:(t + 1) * W_BLOCK].astype(BF16)


def _regroup_in_projection(w_in):
    depth, d, n = w_in.shape
    assert n == N_BLOCKS * W_BLOCK and RET_DK == W_BLOCK
    assert BLK_RET_Q % REGROUP_BLOCKS == 0 and BLK_RET_V % REGROUP_BLOCKS == 0
    return pl.pallas_call(
        _regroup_kernel,
        grid=(depth, N_BLOCKS // REGROUP_BLOCKS),
        in_specs=[pl.BlockSpec((1, d, REGROUP_BLOCKS * W_BLOCK), lambda l, j: (l, 0, j))],
        out_specs=pl.BlockSpec((1, REGROUP_BLOCKS, d, W_BLOCK), lambda l, j: (l, j, 0, 0)),
        out_shape=jax.ShapeDtypeStruct((depth, N_BLOCKS, d, W_BLOCK), BF16),
        name="regroup_in_projection",
    )(w_in)


def _cumsum_rows(x):
    c, w = x.shape
    head_rows = lax.broadcasted_iota(jnp.int32, (SUBLANES, w), 0)
    shift = 1
    while shift < c:
        if shift < SUBLANES:
            moved = pltpu.roll(x, shift, 0)
            first = jnp.where(head_rows >= shift, moved[0:SUBLANES], 0.0)
            moved = jnp.concatenate([first, moved[SUBLANES:]], axis=0)
        else:
            moved = jnp.concatenate([jnp.zeros((shift, w), F32), x[:c - shift]], axis=0)
        x = x + moved
        shift *= 2
    return x


def _neg_abs(x):
    bits = lax.bitcast_convert_type(x, jnp.uint32) | jnp.uint32(0x80000000)
    return lax.bitcast_convert_type(bits, F32)


def _level_operands(q, k, b, b_ref, level):
    c, w = b.shape
    half = 1 << level
    if half >= SUBLANES:
        sel, expo = [], []
        for p in range(c // half):
            rows = slice(p * half, (p + 1) * half)
            mid = (p // 2) * 2 * half + half - 1
            rho = jnp.broadcast_to(b_ref[mid:mid + 1, :], (half, w))
            sel.append((q if p % 2 else k)[rows])
            expo.append(b[rows] - rho if p % 2 else rho - b[rows])
        return jnp.concatenate(sel, axis=0), jnp.exp2(jnp.concatenate(expo, axis=0))
    shape3 = (c // SUBLANES, SUBLANES, w)
    b3 = b.reshape(shape3)
    sub = lax.broadcasted_iota(jnp.int32, (1, SUBLANES, w), 1)
    rho = None
    for p in reversed(range(SUBLANES // (2 * half))):
        mid = p * 2 * half + half - 1
        piece = jnp.broadcast_to(b3[:, mid:mid + 1, :], shape3)
        rho = piece if rho is None else jnp.where(sub < (p + 1) * 2 * half, piece, rho)
    sel = jnp.where(((sub >> level) & 1) == 1, q.reshape(shape3), k.reshape(shape3))
    return sel.reshape(c, w), jnp.exp2(_neg_abs(b3 - rho)).reshape(c, w)


def _hgrn_decay_drop(b_ref):
    c = b_ref.shape[0]
    ends = [b_ref[m - 1:m, :] for m in range(LOCAL_ROWS, c + 1, LOCAL_ROWS)]
    starts = [jnp.zeros_like(ends[0])] + ends[:-1]
    return jnp.max(jnp.concatenate([s - e for s, e in zip(starts, ends)], axis=0))


def _hgrn_chunk(local_blocks, q, k, v, st_ref, lvl_ref, b_ref):
    c = q.shape[0]
    heads = (slice(0, HG_D), slice(HG_D, 2 * HG_D))
    b = b_ref[...]
    zero = jnp.zeros((c, HG_D), BF16)

    def block_diagonal(t):
        return jnp.concatenate([jnp.concatenate([t[:, heads[0]], zero], axis=1),
                                jnp.concatenate([zero, t[:, heads[1]]], axis=1)], axis=0)

    pieces = [slice(r * SUBLANES, (r + 1) * SUBLANES) for r in range(c // SUBLANES)]
    if local_blocks:
        first_level = LOCAL_ROWS.bit_length() - 1
        rho = [jnp.zeros((LOCAL_ROWS, b.shape[1]), F32)]
        rho += [jnp.broadcast_to(b_ref[m - 1:m, :], (LOCAL_ROWS, b.shape[1]))
                for m in range(LOCAL_ROWS, c, LOCAL_ROWS)]
        expo = b - jnp.concatenate(rho, axis=0)
        q_loc = (q * jnp.exp2(expo)).astype(BF16)
        k_loc = (k * jnp.exp2(-expo)).astype(BF16)
        s_local = lax.dot_general(q_loc, block_diagonal(k_loc), NT_DIMS,
                                  preferred_element_type=F32)
        scores = [jnp.where(lvl_ref[rows, :] < first_level, s_local[rows], 0.0) for rows in pieces]
    else:
        first_level = 0
        scores = [jnp.zeros((SUBLANES, 2 * c), F32)] * len(pieces)
    for level in range(first_level, c.bit_length() - 1):
        sel, dec = _level_operands(q, k, b, b_ref, level)
        z = (sel * dec).astype(BF16)
        s_level = lax.dot_general(z, block_diagonal(z), NT_DIMS, preferred_element_type=F32)
        for r, rows in enumerate(pieces):
            if rows.start >> level & 1 or (1 << level) < SUBLANES:
                scores[r] = jnp.where(lvl_ref[rows, :] == level, s_level[rows], scores[r])
    p = jnp.concatenate(scores, axis=0).astype(BF16)
    v16 = v.astype(BF16)
    o = jnp.dot(p, block_diagonal(v16), preferred_element_type=F32)
    if not local_blocks:
        qk = q * k
        diag = [jnp.broadcast_to(jnp.sum(qk[:, hd], axis=-1, keepdims=True), (c, HG_D))
                for hd in heads]
        o = o + jnp.concatenate(diag, axis=1) * v
    st = st_ref[...]
    o = o + lax.dot_general((q * jnp.exp2(b)).astype(BF16), st.astype(BF16), NT_DIMS,
                            preferred_element_type=F32)
    b_last = b_ref[c - 1:c, :]
    k_dec = (k * jnp.exp2(b_last - b)).astype(BF16)
    update = lax.dot_general(v16, k_dec, TN_DIMS, preferred_element_type=F32)
    st_dec = jnp.exp2(b_last)
    for hd in heads:
        st_ref[hd, hd] = st[hd, hd] * st_dec[:, hd] + update[hd, hd]
    return o


def _project(u, w_ref, first, count=1):
    parts = [jnp.dot(u, w_ref[first + t], preferred_element_type=F32) for t in range(count)]
    return parts[0] if count == 1 else jnp.concatenate(parts, axis=1)


def _hgrn_project(p, slot, u_ref, w_ref, lbt_ref, q_ref, k_ref, v_ref, g_ref, b_ref):
    u = u_ref[...]
    tab = jnp.concatenate([lbt_ref[2 * p], lbt_ref[2 * p + 1]], axis=1)
    log_lb, log1m_lb, one_m_lb = tab[0:1], tab[1:2], tab[2:3]
    q_ref[slot] = _silu(_project(u, w_ref, BLK_HG_Q + p))
    zf = _project(u, w_ref, BLK_HG_F + p)
    e = jnp.exp(-jnp.abs(zf))
    one_p_e = 1.0 + e
    a2 = log1m_lb + jnp.minimum(zf, 0.0) - jnp.log(one_p_e)
    lf = jnp.maximum(log_lb, a2) + jnp.log(1.0 + jnp.exp(-jnp.abs(log_lb - a2)))
    lf2 = lf * LOG2_E
    k_ref[slot] = one_m_lb * (jnp.where(zf >= 0.0, e, 1.0) / one_p_e)
    v_ref[slot] = _project(u, w_ref, BLK_HG_V + p)
    g_ref[slot] = _silu(_project(u, w_ref, BLK_HG_G + p))
    drop = None
    for ck in range(lf2.shape[0] // HG_CHUNK):
        b_ref[slot, ck] = _cumsum_rows(lf2[ck * HG_CHUNK:(ck + 1) * HG_CHUNK])
        d = _hgrn_decay_drop(b_ref.at[slot, ck])
        drop = d if drop is None else jnp.maximum(drop, d)
    return drop


def _hgrn_compute(local_blocks, p, slot, lvl_ref, q_ref, k_ref, v_ref, g_ref, b_ref, oa_ref,
                  st_ref):
    tm = q_ref.shape[1]
    for ck in range(tm // HG_CHUNK):
        rows = slice(ck * HG_CHUNK, (ck + 1) * HG_CHUNK)
        o = _hgrn_chunk(local_blocks, q_ref[slot, rows, :], k_ref[slot, rows, :],
                        v_ref[slot, rows, :], st_ref.at[p], lvl_ref, b_ref.at[slot, ck])
        for j in range(2):
            cols = slice(j * HG_D, (j + 1) * HG_D)
            oj = o[:, cols]
            oj = oj * lax.rsqrt(jnp.mean(oj * oj, axis=-1, keepdims=True) + HEAD_NORM_EPS)
            oa_ref[p, rows, cols] = (oj * g_ref[slot, rows, cols]).astype(BF16)


def _retention_project(h, slot, u_ref, w_ref, cs_ref, qdec_ref, kdec_ref,
                       q_ref, qd_ref, k_ref, kd_ref, v_ref, g_ref):
    half = RET_DK // 2
    u = u_ref[...]
    cos, sin = cs_ref[0, :, 0:half], cs_ref[0, :, half:RET_DK]

    def rotated(block, dec_ref, plain_ref, scaled_ref):
        t = _project(u, w_ref, block)
        t1, t2 = t[:, 0:half], t[:, half:RET_DK]
        r1, r2 = t1 * cos - t2 * sin, t1 * sin + t2 * cos
        dec = dec_ref[h]
        plain_ref[slot] = jnp.concatenate([r1, r2], axis=1).astype(BF16)
        scaled_ref[slot] = jnp.concatenate([r1 * dec, r2 * dec], axis=1).astype(BF16)

    rotated(BLK_RET_Q + h, qdec_ref, q_ref, qd_ref)
    rotated(BLK_RET_K + h, kdec_ref, k_ref, kd_ref)
    wide = RET_DV // W_BLOCK
    v_ref[slot] = _project(u, w_ref, BLK_RET_V + wide * h, wide).astype(BF16)
    g_ref[slot] = _silu(_project(u, w_ref, BLK_RET_G + wide * h, wide))


def _retention_compute(h, slot, dmask_ref, cdec_ref, q_ref, qd_ref, k_ref, kd_ref, v_ref, g_ref,
                       ob_ref, st_ref):
    v16 = v_ref[slot]
    scores = lax.dot_general(q_ref[slot], k_ref[slot], NT_DIMS,
                             preferred_element_type=F32) * dmask_ref[h]
    st = st_ref[h]
    o = (jnp.dot(scores.astype(BF16), v16, preferred_element_type=F32)
         + jnp.dot(qd_ref[slot], st.astype(BF16), preferred_element_type=F32))
    st_ref[h] = cdec_ref[h][0:1, :] * st + lax.dot_general(kd_ref[slot], v16, TN_DIMS,
                                                           preferred_element_type=F32)
    o = o - jnp.mean(o, axis=-1, keepdims=True)
    o = o * lax.rsqrt(jnp.mean(o * o, axis=-1, keepdims=True) + HEAD_NORM_EPS)
    ob_ref[h] = (o * g_ref[slot]).astype(BF16)


def _mixer_kernel(alpha, x_ref, mod_ref, cs_ref, lbt_ref, lvl_ref, dmask_ref, qdec_ref, kdec_ref,
                  cdec_ref, w_ref, wpa_ref, wpb_ref, wo_ref, lng_ref, lnb_ref,
                  o_ref, u_ref, hq_ref, hk_ref, hv_ref, hg_ref,
                  rq_ref, rqd_ref, rk_ref, rkd_ref, rv_ref, rg_ref,
                  oa_ref, ob_ref, sth_ref, str_ref, b_ref):
    @pl.when(pl.program_id(1) == 0)
    def _():
        sth_ref[...] = jnp.zeros_like(sth_ref)
        str_ref[...] = jnp.zeros_like(str_ref)

    x = x_ref[0]
    shift, scale, gate = mod_ref[0, 0:1, :], mod_ref[0, 1:2, :], mod_ref[0, 2:3, :]
    u_ref[...] = (x * (1.0 + scale) + shift).astype(BF16)

    assert HG_PAIRS == RET_HEADS
    hgrn_act = (hq_ref, hk_ref, hv_ref, hg_ref, b_ref)
    ret_act = (rq_ref, rqd_ref, rk_ref, rkd_ref, rv_ref, rg_ref)
    drop = None
    for i in range(HG_PAIRS):
        d = _hgrn_project(i, i, u_ref, w_ref, lbt_ref, *hgrn_act)
        drop = d if drop is None else jnp.maximum(drop, d)

    def rest(local_blocks):
        for i in range(HG_PAIRS):
            _hgrn_compute(local_blocks, i, i, lvl_ref, *hgrn_act, oa_ref, sth_ref)
            _retention_project(i, i % 2, u_ref, w_ref, cs_ref, qdec_ref, kdec_ref, *ret_act)
            _retention_compute(i, i % 2, dmask_ref, cdec_ref, *ret_act, ob_ref, str_ref)
        ya = jnp.dot(oa_ref[0], wpa_ref[0], preferred_element_type=F32)
        yb = jnp.dot(ob_ref[0], wpb_ref[0], preferred_element_type=F32)
        for i in range(1, HG_PAIRS):
            ya = ya + jnp.dot(oa_ref[i], wpa_ref[i], preferred_element_type=F32)
            yb = yb + jnp.dot(ob_ref[i], wpb_ref[i], preferred_element_type=F32)
        gates = _project(u_ref[...], w_ref, BLK_GATES, 2 * D_MODEL // W_BLOCK)
        y = (jax.nn.sigmoid(gates[:, 0:D_MODEL]) * ya
             + jax.nn.sigmoid(gates[:, D_MODEL:2 * D_MODEL]) * yb)
        y = jnp.dot(y.astype(BF16), wo_ref[...], preferred_element_type=F32)
        o_ref[0] = _layer_norm(alpha * x_ref[0] + (1.0 + gate) * y, lng_ref[...], lnb_ref[...])

    lax.cond(drop <= LOCAL_MAX_DROP, functools.partial(rest, True), functools.partial(rest, False))


def _mixer_tables(tm):
    i = np.arange(HG_CHUNK)
    differ = i[:, None] ^ i[None, :]
    level = np.where(i[:, None] > i[None, :], np.floor(np.log2(np.maximum(differ, 1))), LEVEL_NONE)
    level[i, i] = LEVEL_DIAGONAL
    log_gamma = np.log1p(-np.exp2(-5.0 - np.arange(RET_HEADS)))[:, None, None]
    r = np.arange(tm, dtype=np.float64)
    diff = r[:, None] - r[None, :]
    scale = RET_DK ** -0.5
    dmask = np.where(diff >= 0, np.exp(log_gamma * np.maximum(diff, 0.0)), 0.0) * scale
    ones = np.ones((1, 1, RET_DK // 2))
    qdec = np.exp(log_gamma * (r[None, :, None] + 1.0)) * ones
    kdec = np.exp(log_gamma * (tm - 1.0 - r[None, :, None])) * scale * ones
    cdec = np.exp(log_gamma * tm) * np.ones((1, SUBLANES, RET_DV))
    level = np.tile(level, (1, 2))
    return (jnp.asarray(level, jnp.int32), jnp.asarray(dmask, F32), jnp.asarray(qdec, F32),
            jnp.asarray(kdec, F32), jnp.asarray(cdec, F32))


def _mixer(alpha, layer, x, mod, cos_sin, lbt, w_blocks, wpa, wpb, wo, ln_g, ln_b):
    batch, seq, d = x.shape
    tm = min(MIX_TM, seq)
    tables = _mixer_tables(tm)
    tile = pl.BlockSpec((1, tm, d), lambda b, s: (b, s, 0))
    consts = (lbt,) + tables
    weights = (wpa, wpb, wo, ln_g, ln_b)
    layer_blocks = pl.BlockSpec((None,) + w_blocks.shape[1:], lambda b, s: (layer, 0, 0, 0),
                                pipeline_mode=pl.Buffered(1))
    return pl.pallas_call(
        functools.partial(_mixer_kernel, alpha),
        grid=(batch, seq // tm),
        in_specs=[tile,
                  pl.BlockSpec((1, 6, d), lambda b, s: (b, 0, 0)),
                  pl.BlockSpec((1, tm, RET_DK), lambda b, s: (b, s, 0))]
                 + [_resident(a.shape) for a in consts] + [layer_blocks]
                 + [_resident(a.shape) for a in weights],
        out_specs=tile,
        out_shape=jax.ShapeDtypeStruct(x.shape, F32),
        scratch_shapes=[pltpu.VMEM((tm, d), BF16)]
                       + [pltpu.VMEM((HG_PAIRS, tm, 2 * HG_D), F32)] * 4
                       + [pltpu.VMEM((2, tm, RET_DK), BF16)] * 4
                       + [pltpu.VMEM((2, tm, RET_DV), BF16),
                          pltpu.VMEM((2, tm, RET_DV), F32),
                          pltpu.VMEM((HG_PAIRS, tm, 2 * HG_D), BF16),
                          pltpu.VMEM((RET_HEADS, tm, RET_DV), BF16),
                          pltpu.VMEM((HG_PAIRS, 2 * HG_D, 2 * HG_D), F32),
                          pltpu.VMEM((RET_HEADS, RET_DK, RET_DV), F32),
                          pltpu.VMEM((HG_PAIRS, tm // HG_CHUNK, HG_CHUNK, 2 * HG_D), F32)],
        compiler_params=pltpu.CompilerParams(
            dimension_semantics=("arbitrary", "arbitrary"), vmem_limit_bytes=VMEM_LIMIT),
        name="mixer_sublayer",
    )(x, mod, cos_sin, *consts, w_blocks, *weights)


def _mlp_kernel(alpha, x_ref, mod_ref, wup_ref, bup_ref, wdn_ref, bdn_ref, lng_ref, lnb_ref,
                o_ref, h_ref):
    x = x_ref[0]
    shift, scale, gate = mod_ref[0, 3:4, :], mod_ref[0, 4:5, :], mod_ref[0, 5:6, :]
    u = (x * (1.0 + scale) + shift).astype(BF16)
    for j in range(D_FF // D_MODEL):
        cols = slice(j * D_MODEL, (j + 1) * D_MODEL)
        hj = jnp.dot(u, wup_ref[:, cols], preferred_element_type=F32) + bup_ref[:, cols]
        h_ref[:, cols] = jnp.square(jnp.maximum(hj, 0.0)).astype(BF16)
    y = jnp.dot(h_ref[...], wdn_ref[...], preferred_element_type=F32) + bdn_ref[...]
    o_ref[0] = _layer_norm(alpha * x + (1.0 + gate) * y, lng_ref[...], lnb_ref[...])


def _mlp(alpha, x, mod, wup, bup, wdn, bdn, ln_g, ln_b):
    batch, seq, d = x.shape
    tm = min(MLP_TM, seq)
    tile = pl.BlockSpec((1, tm, d), lambda b, s: (b, s, 0))
    return pl.pallas_call(
        functools.partial(_mlp_kernel, alpha),
        grid=(batch, seq // tm),
        in_specs=[tile,
                  pl.BlockSpec((1, 6, d), lambda b, s: (b, 0, 0)),
                  _resident(wup.shape), _resident(bup.shape), _resident(wdn.shape),
                  _resident(bdn.shape), _resident(ln_g.shape), _resident(ln_b.shape)],
        out_specs=tile,
        out_shape=jax.ShapeDtypeStruct(x.shape, F32),
        scratch_shapes=[pltpu.VMEM((tm, D_FF), BF16)],
        compiler_params=pltpu.CompilerParams(
            dimension_semantics=("arbitrary", "arbitrary"), vmem_limit_bytes=VMEM_LIMIT),
        name="mlp_sublayer",
    )(x, mod, wup, bup, wdn, bdn, ln_g, ln_b)


def kernel(x, c, positions, lb_logits, w_ada, b_ada, w_in, w_pa, w_pb, w_o, ln1_g, ln1_b,
           w_up, b_up, w_down, b_down, ln2_g, ln2_b):
    depth = w_in.shape[0]
    batch, seq, d = x.shape
    alpha = (2 * depth) ** 0.25
    mod = _modulation(c, w_ada, b_ada).reshape(depth, batch, 6, d)
    lbt = _lower_bound_tables(lb_logits)
    lbt = lbt.reshape(depth, 8, HG_HEADS, HG_D).transpose(0, 2, 1, 3)
    theta = (ROPE_BASE ** (-jnp.linspace(0.0, 1.0, RET_DK // 2, dtype=F32))).reshape(1, -1)
    cos_sin = _rope_table(positions, theta)
    w_blocks = _regroup_in_projection(w_in)
    for l in range(depth):
        x = _mixer(alpha, l, x, mod[l], cos_sin, lbt[l], w_blocks,
                   w_pa[l].astype(BF16).reshape(HG_PAIRS, 2 * HG_D, d),
                   w_pb[l].astype(BF16).reshape(RET_HEADS, RET_DV, d),
                   w_o[l].astype(BF16), ln1_g[l].reshape(1, d), ln1_b[l].reshape(1, d))
        x = _mlp(alpha, x, mod[l], w_up[l].astype(BF16), b_up[l].reshape(1, -1),
                 w_down[l].astype(BF16), b_down[l].reshape(1, d),
                 ln2_g[l].reshape(1, d), ln2_b[l].reshape(1, d))
    return x
```

```python
import functools

import jax
import jax.numpy as jnp
import numpy as np
from jax import lax
from jax.experimental import pallas as pl
from jax.experimental.pallas import tpu as pltpu

F32 = jnp.float32
BF16 = jnp.bfloat16

D_MODEL = 1024
HG_HEADS = 8
SUBLANES = 8
HG_D = 128
HG_PAIRS = HG_HEADS // 2
RET_HEADS = 4
RET_DK = 256
RET_DV = 512
W_BLOCK = 256
BLK_HG_Q, BLK_HG_F, BLK_HG_V, BLK_HG_G = 0, 4, 8, 12
BLK_RET_Q, BLK_RET_K, BLK_RET_V, BLK_RET_G = 16, 20, 24, 32
BLK_GATES = 40
N_BLOCKS = 48
REGROUP_BLOCKS = 4
D_FF = 4 * D_MODEL
ROPE_BASE = 10000.0
LN_EPS = 1e-5
HEAD_NORM_EPS = 1e-6
LOG2_E = 1.4426950408889634
LOCAL_ROWS = 64
LOCAL_MAX_DROP = 110.0
LEVEL_DIAGONAL, LEVEL_NONE = -2, 99

MIX_TM = 256
HG_CHUNK = 128
MLP_TM = 1024
MLP_SUB = 256
ROPE_ROWS = 1024
VMEM_LIMIT = 60 * 1024 * 1024

NT_DIMS = (((1,), (1,)), ((), ()))
TN_DIMS = (((0,), (0,)), ((), ()))


def _resident(shape):
    n = len(shape)
    return pl.BlockSpec(shape, lambda *_: (0,) * n, pipeline_mode=pl.Buffered(1))


def _layer_norm(r, g, b):
    mu = jnp.mean(r, axis=-1, keepdims=True)
    d = r - mu
    var = jnp.mean(d * d, axis=-1, keepdims=True)
    return d * lax.rsqrt(var + LN_EPS) * g + b


def _silu(z):
    return z * jax.nn.sigmoid(z)


def _mod_kernel(c_ref, w_ref, b_ref, o_ref):
    cond = _silu(c_ref[...])
    o_ref[0] = jnp.dot(cond, w_ref[0], preferred_element_type=F32,
                       precision=lax.Precision.HIGHEST) + b_ref[0]


def _modulation(c, w_ada, b_ada):
    depth, d, n = w_ada.shape
    batch = c.shape[0]
    nb = n // d
    return pl.pallas_call(
        _mod_kernel,
        grid=(depth, nb),
        in_specs=[pl.BlockSpec((batch, d), lambda l, j: (0, 0)),
                  pl.BlockSpec((1, d, d), lambda l, j: (l, 0, j)),
                  pl.BlockSpec((1, 1, d), lambda l, j: (l, 0, j))],
        out_specs=pl.BlockSpec((1, batch, d), lambda l, j: (l, 0, j)),
        out_shape=jax.ShapeDtypeStruct((depth, batch, n), F32),
        name="ada_modulation",
    )(c, w_ada, b_ada.reshape(depth, 1, n))


def _rope_kernel(pos_ref, theta_ref, o_ref):
    ang = pos_ref[0].astype(F32) * theta_ref[...]
    half = RET_DK // 2
    o_ref[0, :, 0:half] = jnp.cos(ang)
    o_ref[0, :, half:RET_DK] = jnp.sin(ang)


def _rope_table(positions, theta):
    batch, seq = positions.shape
    rows = min(ROPE_ROWS, seq)
    return pl.pallas_call(
        _rope_kernel,
        grid=(batch, seq // rows),
        in_specs=[pl.BlockSpec((1, rows, 1), lambda b, s: (b, s, 0)),
                  pl.BlockSpec(theta.shape, lambda b, s: (0, 0))],
        out_specs=pl.BlockSpec((1, rows, RET_DK), lambda b, s: (b, s, 0)),
        out_shape=jax.ShapeDtypeStruct((batch, seq, RET_DK), F32),
        name="rope_table",
    )(positions.reshape(batch, seq, 1), theta)


def _lb_kernel(depth, logit_ref, o_ref):
    z = logit_ref[...]
    e = jnp.exp(z - jnp.max(z, axis=0, keepdims=True))
    sm = e / jnp.sum(e, axis=0, keepdims=True)
    o_ref[...] = jnp.zeros_like(o_ref)
    cum = jnp.zeros_like(sm[0:1])
    for l in range(depth):
        cum = cum + sm[l:l + 1]
        lb = cum - sm[0:1]
        o_ref[l, 0:1, :] = jnp.log(lb)
        o_ref[l, 1:2, :] = jnp.log1p(-lb)
        o_ref[l, 2:3, :] = 1.0 - lb


def _lower_bound_tables(lb_logits):
    depth, n = lb_logits.shape
    return pl.pallas_call(
        functools.partial(_lb_kernel, depth),
        out_shape=jax.ShapeDtypeStruct((depth, 8, n), F32),
        name="hgrn_lower_bounds",
    )(lb_logits.astype(F32))


def _regroup_kernel(w_ref, o_ref):
    j = pl.program_id(1) * REGROUP_BLOCKS
    rotary = (j >= BLK_RET_Q) & (j < BLK_RET_V)

    @pl.when(rotary)
    def _():
        src = lax.broadcasted_iota(jnp.int32, (W_BLOCK, W_BLOCK), 0)
        dst = lax.broadcasted_iota(jnp.int32, (W_BLOCK, W_BLOCK), 1)
        half = W_BLOCK // 2
        want = jnp.where(dst < half, 2 * dst, 2 * (dst - half) + 1)
        perm = (src == want).astype(BF16)
        for t in range(REGROUP_BLOCKS):
            blk = w_ref[0, :, t * W_BLOCK:(t + 1) * W_BLOCK].astype(BF16)
            o_ref[0, t] = jnp.dot(blk, perm, preferred_element_type=F32).astype(BF16)

    @pl.when(jnp.logical_not(rotary))
    def _():
        for t in range(REGROUP_BLOCKS):
            o_ref[0, t] = w_ref[0, :, t * W_BLOCK:(t + 1) * W_BLOCK].astype(BF16)


def _regroup_in_projection(w_in):
    depth, d, n = w_in.shape
    assert n == N_BLOCKS * W_BLOCK and RET_DK == W_BLOCK
    assert BLK_RET_Q % REGROUP_BLOCKS == 0 and BLK_RET_V % REGROUP_BLOCKS == 0
    return pl.pallas_call(
        _regroup_kernel,
        grid=(depth, N_BLOCKS // REGROUP_BLOCKS),
        in_specs=[pl.BlockSpec((1, d, REGROUP_BLOCKS * W_BLOCK), lambda l, j: (l, 0, j))],
        out_specs=pl.BlockSpec((1, REGROUP_BLOCKS, d, W_BLOCK), lambda l, j: (l, j, 0, 0)),
        out_shape=jax.ShapeDtypeStruct((depth, N_BLOCKS, d, W_BLOCK), BF16),
        name="regroup_in_projection",
    )(w_in)


def _cumsum_rows(x):
    c, w = x.shape
    head_rows = lax.broadcasted_iota(jnp.int32, (SUBLANES, w), 0)
    shift = 1
    while shift < c:
        if shift < SUBLANES:
            moved = pltpu.roll(x, shift, 0)
            first = jnp.where(head_rows >= shift, moved[0:SUBLANES], 0.0)
            moved = jnp.concatenate([first, moved[SUBLANES:]], axis=0)
        else:
            moved = jnp.concatenate([jnp.zeros((shift, w), F32), x[:c - shift]], axis=0)
        x = x + moved
        shift *= 2
    return x


def _neg_abs(x):
    bits = lax.bitcast_convert_type(x, jnp.uint32) | jnp.uint32(0x80000000)
    return lax.bitcast_convert_type(bits, F32)


def _level_operands(q, k, b, b_ref, level):
    c, w = b.shape
    half = 1 << level
    if half >= SUBLANES:
        sel, expo = [], []
        for p in range(c // half):
            rows = slice(p * half, (p + 1) * half)
            mid = (p // 2) * 2 * half + half - 1
            rho = jnp.broadcast_to(b_ref[mid:mid + 1, :], (half, w))
            sel.append((q if p % 2 else k)[rows])
            expo.append(b[rows] - rho if p % 2 else rho - b[rows])
        return jnp.concatenate(sel, axis=0), jnp.exp2(jnp.concatenate(expo, axis=0))
    shape3 = (c // SUBLANES, SUBLANES, w)
    b3 = b.reshape(shape3)
    sub = lax.broadcasted_iota(jnp.int32, (1, SUBLANES, w), 1)
    rho = None
    for p in reversed(range(SUBLANES // (2 * half))):
        mid = p * 2 * half + half - 1
        piece = jnp.broadcast_to(b3[:, mid:mid + 1, :], shape3)
        rho = piece if rho is None else jnp.where(sub < (p + 1) * 2 * half, piece, rho)
    sel = jnp.where(((sub >> level) & 1) == 1, q.reshape(shape3), k.reshape(shape3))
    return sel.reshape(c, w), jnp.exp2(_neg_abs(b3 - rho)).reshape(c, w)


def _hgrn_decay_drop(b_ref):
    c = b_ref.shape[0]
    ends = [b_ref[m - 1:m, :] for m in range(LOCAL_ROWS, c + 1, LOCAL_ROWS)]
    starts = [jnp.zeros_like(ends[0])] + ends[:-1]
    return jnp.max(jnp.concatenate([s - e for s, e in zip(starts, ends)], axis=0))


def _hgrn_chunk(local_blocks, q, k, v, st_ref, lvl_ref, b_ref):
    c = q.shape[0]
    heads = (slice(0, HG_D), slice(HG_D, 2 * HG_D))
    b = b_ref[...]
    zero = jnp.zeros((c, HG_D), BF16)

    def block_diagonal(t):
        return jnp.concatenate([jnp.concatenate([t[:, heads[0]], zero], axis=1),
                                jnp.concatenate([zero, t[:, heads[1]]], axis=1)], axis=0)

    pieces = [slice(r * SUBLANES, (r + 1) * SUBLANES) for r in range(c // SUBLANES)]
    if local_blocks:
        first_level = LOCAL_ROWS.bit_length() - 1
        rho = [jnp.zeros((LOCAL_ROWS, b.shape[1]), F32)]
        rho += [jnp.broadcast_to(b_ref[m - 1:m, :], (LOCAL_ROWS, b.shape[1]))
                for m in range(LOCAL_ROWS, c, LOCAL_ROWS)]
        expo = b - jnp.concatenate(rho, axis=0)
        q_loc = (q * jnp.exp2(expo)).astype(BF16)
        k_loc = (k * jnp.exp2(-expo)).astype(BF16)
        s_local = lax.dot_general(q_loc, block_diagonal(k_loc), NT_DIMS,
                                  preferred_element_type=F32)
        scores = [jnp.where(lvl_ref[rows, :] < first_level, s_local[rows], 0.0) for rows in pieces]
    else:
        first_level = 0
        scores = [jnp.zeros((SUBLANES, 2 * c), F32)] * len(pieces)
    for level in range(first_level, c.bit_length() - 1):
        sel, dec = _level_operands(q, k, b, b_ref, level)
        z = (sel * dec).astype(BF16)
        s_level = lax.dot_general(z, block_diagonal(z), NT_DIMS, preferred_element_type=F32)
        for r, rows in enumerate(pieces):
            if rows.start >> level & 1 or (1 << level) < SUBLANES:
                scores[r] = jnp.where(lvl_ref[rows, :] == level, s_level[rows], scores[r])
    p = jnp.concatenate(scores, axis=0).astype(BF16)
    v16 = v.astype(BF16)
    o = jnp.dot(p, block_diagonal(v16), preferred_element_type=F32)
    if not local_blocks:
        qk = q * k
        diag = [jnp.broadcast_to(jnp.sum(qk[:, hd], axis=-1, keepdims=True), (c, HG_D))
                for hd in heads]
        o = o + jnp.concatenate(diag, axis=1) * v
    st = st_ref[...]
    o = o + lax.dot_general((q * jnp.exp2(b)).astype(BF16), st.astype(BF16), NT_DIMS,
                            preferred_element_type=F32)
    b_last = b_ref[c - 1:c, :]
    k_dec = (k * jnp.exp2(b_last - b)).astype(BF16)
    update = lax.dot_general(v16, k_dec, TN_DIMS, preferred_element_type=F32)
    st_dec = jnp.exp2(b_last)
    for hd in heads:
        st_ref[hd, hd] = st[hd, hd] * st_dec[:, hd] + update[hd, hd]
    return o


def _project(u, w_ref, first, count=1):
    parts = [jnp.dot(u, w_ref[first + t], preferred_element_type=F32) for t in range(count)]
    return parts[0] if count == 1 else jnp.concatenate(parts, axis=1)


def _hgrn_project(p, slot, u_ref, w_ref, lbt_ref, q_ref, k_ref, v_ref, g_ref, b_ref):
    u = u_ref[...]
    tab = jnp.concatenate([lbt_ref[2 * p], lbt_ref[2 * p + 1]], axis=1)
    log_lb, log1m_lb, one_m_lb = tab[0:1], tab[1:2], tab[2:3]
    q_ref[slot] = _silu(_project(u, w_ref, BLK_HG_Q + p))
    zf = _project(u, w_ref, BLK_HG_F + p)
    e = jnp.exp(-jnp.abs(zf))
    one_p_e = 1.0 + e
    a2 = log1m_lb + jnp.minimum(zf, 0.0) - jnp.log(one_p_e)
    lf = jnp.maximum(log_lb, a2) + jnp.log(1.0 + jnp.exp(-jnp.abs(log_lb - a2)))
    lf2 = lf * LOG2_E
    k_ref[slot] = one_m_lb * (jnp.where(zf >= 0.0, e, 1.0) / one_p_e)
    v_ref[slot] = _project(u, w_ref, BLK_HG_V + p)
    g_ref[slot] = _silu(_project(u, w_ref, BLK_HG_G + p))
    drop = None
    for ck in range(lf2.shape[0] // HG_CHUNK):
        b_ref[slot, ck] = _cumsum_rows(lf2[ck * HG_CHUNK:(ck + 1) * HG_CHUNK])
        d = _hgrn_decay_drop(b_ref.at[slot, ck])
        drop = d if drop is None else jnp.maximum(drop, d)
    return drop


def _hgrn_compute(local_blocks, p, slot, lvl_ref, q_ref, k_ref, v_ref, g_ref, b_ref, oa_ref,
                  st_ref):
    tm = q_ref.shape[1]
    for ck in range(tm // HG_CHUNK):
        rows = slice(ck * HG_CHUNK, (ck + 1) * HG_CHUNK)
        o = _hgrn_chunk(local_blocks, q_ref[slot, rows, :], k_ref[slot, rows, :],
                        v_ref[slot, rows, :], st_ref.at[p], lvl_ref, b_ref.at[slot, ck])
        for j in range(2):
            cols = slice(j * HG_D, (j + 1) * HG_D)
            oj = o[:, cols]
            oj = oj * lax.rsqrt(jnp.mean(oj * oj, axis=-1, keepdims=True) + HEAD_NORM_EPS)
            oa_ref[p, rows, cols] = (oj * g_ref[slot, rows, cols]).astype(BF16)


def _retention_project(h, slot, u_ref, w_ref, cs_ref, qdec_ref, kdec_ref,
                       q_ref, qd_ref, k_ref, kd_ref, v_ref, g_ref):
    half = RET_DK // 2
    u = u_ref[...]
    cos, sin = cs_ref[0, :, 0:half], cs_ref[0, :, half:RET_DK]

    def rotated(block, dec_ref, plain_ref, scaled_ref):
        t = _project(u, w_ref, block)
        t1, t2 = t[:, 0:half], t[:, half:RET_DK]
        r1, r2 = t1 * cos - t2 * sin, t1 * sin + t2 * cos
        dec = dec_ref[h]
        plain_ref[slot] = jnp.concatenate([r1, r2], axis=1).astype(BF16)
        scaled_ref[slot] = jnp.concatenate([r1 * dec, r2 * dec], axis=1).astype(BF16)

    rotated(BLK_RET_Q + h, qdec_ref, q_ref, qd_ref)
    rotated(BLK_RET_K + h, kdec_ref, k_ref, kd_ref)
    wide = RET_DV // W_BLOCK
    v_ref[slot] = _project(u, w_ref, BLK_RET_V + wide * h, wide).astype(BF16)
    g_ref[slot] = _silu(_project(u, w_ref, BLK_RET_G + wide * h, wide))


def _retention_compute(h, slot, dmask_ref, cdec_ref, q_ref, qd_ref, k_ref, kd_ref, v_ref, g_ref,
                       ob_ref, st_ref):
    v16 = v_ref[slot]
    scores = lax.dot_general(q_ref[slot], k_ref[slot], NT_DIMS,
                             preferred_element_type=F32) * dmask_ref[h]
    st = st_ref[h]
    o = (jnp.dot(scores.astype(BF16), v16, preferred_element_type=F32)
         + jnp.dot(qd_ref[slot], st.astype(BF16), preferred_element_type=F32))
    st_ref[h] = cdec_ref[h][0:1, :] * st + lax.dot_general(kd_ref[slot], v16, TN_DIMS,
                                                           preferred_element_type=F32)
    o = o - jnp.mean(o, axis=-1, keepdims=True)
    o = o * lax.rsqrt(jnp.mean(o * o, axis=-1, keepdims=True) + HEAD_NORM_EPS)
    ob_ref[h] = (o * g_ref[slot]).astype(BF16)


def _mixer_kernel(alpha, x_ref, mod_ref, cs_ref, lbt_ref, lvl_ref, dmask_ref, qdec_ref, kdec_ref,
                  cdec_ref, w_ref, wpa_ref, wpb_ref, wo_ref, lng_ref, lnb_ref,
                  o_ref, u_ref, hq_ref, hk_ref, hv_ref, hg_ref,
                  rq_ref, rqd_ref, rk_ref, rkd_ref, rv_ref, rg_ref,
                  oa_ref, ob_ref, sth_ref, str_ref, b_ref):
    @pl.when(pl.program_id(1) == 0)
    def _():
        sth_ref[...] = jnp.zeros_like(sth_ref)
        str_ref[...] = jnp.zeros_like(str_ref)

    x = x_ref[0]
    shift, scale, gate = mod_ref[0, 0:1, :], mod_ref[0, 1:2, :], mod_ref[0, 2:3, :]
    u_ref[...] = (x * (1.0 + scale) + shift).astype(BF16)

    assert HG_PAIRS == RET_HEADS
    hgrn_act = (hq_ref, hk_ref, hv_ref, hg_ref, b_ref)
    ret_act = (rq_ref, rqd_ref, rk_ref, rkd_ref, rv_ref, rg_ref)
    drop = None
    for i in range(HG_PAIRS):
        d = _hgrn_project(i, i, u_ref, w_ref, lbt_ref, *hgrn_act)
        drop = d if drop is None else jnp.maximum(drop, d)

    def rest(local_blocks):
        for i in range(HG_PAIRS):
            _hgrn_compute(local_blocks, i, i, lvl_ref, *hgrn_act, oa_ref, sth_ref)
            _retention_project(i, i % 2, u_ref, w_ref, cs_ref, qdec_ref, kdec_ref, *ret_act)
            _retention_compute(i, i % 2, dmask_ref, cdec_ref, *ret_act, ob_ref, str_ref)
        ya = jnp.dot(oa_ref[0], wpa_ref[0], preferred_element_type=F32)
        yb = jnp.dot(ob_ref[0], wpb_ref[0], preferred_element_type=F32)
        for i in range(1, HG_PAIRS):
            ya = ya + jnp.dot(oa_ref[i], wpa_ref[i], preferred_element_type=F32)
            yb = yb + jnp.dot(ob_ref[i], wpb_ref[i], preferred_element_type=F32)
        gates = _project(u_ref[...], w_ref, BLK_GATES, 2 * D_MODEL // W_BLOCK)
        y = (jax.nn.sigmoid(gates[:, 0:D_MODEL]) * ya
             + jax.nn.sigmoid(gates[:, D_MODEL:2 * D_MODEL]) * yb)
        y = jnp.dot(y.astype(BF16), wo_ref[...], preferred_element_type=F32)
        o_ref[0] = _layer_norm(alpha * x_ref[0] + (1.0 + gate) * y, lng_ref[...], lnb_ref[...])

    lax.cond(drop <= LOCAL_MAX_DROP, functools.partial(rest, True), functools.partial(rest, False))


def _mixer_tables(tm):
    i = np.arange(HG_CHUNK)
    differ = i[:, None] ^ i[None, :]
    level = np.where(i[:, None] > i[None, :], np.floor(np.log2(np.maximum(differ, 1))), LEVEL_NONE)
    level[i, i] = LEVEL_DIAGONAL
    log_gamma = np.log1p(-np.exp2(-5.0 - np.arange(RET_HEADS)))[:, None, None]
    r = np.arange(tm, dtype=np.float64)
    diff = r[:, None] - r[None, :]
    scale = RET_DK ** -0.5
    dmask = np.where(diff >= 0, np.exp(log_gamma * np.maximum(diff, 0.0)), 0.0) * scale
    ones = np.ones((1, 1, RET_DK // 2))
    qdec = np.exp(log_gamma * (r[None, :, None] + 1.0)) * ones
    kdec = np.exp(log_gamma * (tm - 1.0 - r[None, :, None])) * scale * ones
    cdec = np.exp(log_gamma * tm) * np.ones((1, SUBLANES, RET_DV))
    level = np.tile(level, (1, 2))
    return (jnp.asarray(level, jnp.int32), jnp.asarray(dmask, F32), jnp.asarray(qdec, F32),
            jnp.asarray(kdec, F32), jnp.asarray(cdec, F32))


def _mixer(alpha, layer, x, mod, cos_sin, lbt, w_blocks, wpa, wpb, wo, ln_g, ln_b):
    batch, seq, d = x.shape
    tm = min(MIX_TM, seq)
    tables = _mixer_tables(tm)
    tile = pl.BlockSpec((1, tm, d), lambda b, s: (b, s, 0))
    consts = (lbt,) + tables
    weights = (wpa, wpb, wo, ln_g, ln_b)
    layer_blocks = pl.BlockSpec((None,) + w_blocks.shape[1:], lambda b, s: (layer, 0, 0, 0),
                                pipeline_mode=pl.Buffered(1))
    return pl.pallas_call(
        functools.partial(_mixer_kernel, alpha),
        grid=(batch, seq // tm),
        in_specs=[tile,
                  pl.BlockSpec((1, 6, d), lambda b, s: (b, 0, 0)),
                  pl.BlockSpec((1, tm, RET_DK), lambda b, s: (b, s, 0))]
                 + [_resident(a.shape) for a in consts] + [layer_blocks]
                 + [_resident(a.shape) for a in weights],
        out_specs=tile,
        out_shape=jax.ShapeDtypeStruct(x.shape, F32),
        scratch_shapes=[pltpu.VMEM((tm, d), BF16)]
                       + [pltpu.VMEM((HG_PAIRS, tm, 2 * HG_D), F32)] * 4
                       + [pltpu.VMEM((2, tm, RET_DK), BF16)] * 4
                       + [pltpu.VMEM((2, tm, RET_DV), BF16),
                          pltpu.VMEM((2, tm, RET_DV), F32),
                          pltpu.VMEM((HG_PAIRS, tm, 2 * HG_D), BF16),
                          pltpu.VMEM((RET_HEADS, tm, RET_DV), BF16),
                          pltpu.VMEM((HG_PAIRS, 2 * HG_D, 2 * HG_D), F32),
                          pltpu.VMEM((RET_HEADS, RET_DK, RET_DV), F32),
                          pltpu.VMEM((HG_PAIRS, tm // HG_CHUNK, HG_CHUNK, 2 * HG_D), F32)],
        compiler_params=pltpu.CompilerParams(
            dimension_semantics=("arbitrary", "arbitrary"), vmem_limit_bytes=VMEM_LIMIT),
        name="mixer_sublayer",
    )(x, mod, cos_sin, *consts, w_blocks, *weights)


def _mlp_kernel(alpha, x_ref, mod_ref, wup_ref, bup_ref, wdn_ref, bdn_ref, lng_ref, lnb_ref,
                o_ref, h_ref):
    shift, scale, gate = mod_ref[0, 3:4, :], mod_ref[0, 4:5, :], mod_ref[0, 5:6, :]
    for t in range(x_ref.shape[1] // MLP_SUB):
        rows = slice(t * MLP_SUB, (t + 1) * MLP_SUB)
        x = x_ref[0, rows, :]
        u = (x * (1.0 + scale) + shift).astype(BF16)
        for j in range(D_FF // D_MODEL):
            cols = slice(j * D_MODEL, (j + 1) * D_MODEL)
            hj = jnp.dot(u, wup_ref[:, cols], preferred_element_type=F32) + bup_ref[:, cols]
            h_ref[t, :, cols] = jnp.square(jnp.maximum(hj, 0.0)).astype(BF16)
        y = jnp.dot(h_ref[t], wdn_ref[...], preferred_element_type=F32) + bdn_ref[...]
        o_ref[0, rows, :] = _layer_norm(alpha * x + (1.0 + gate) * y, lng_ref[...], lnb_ref[...])


def _mlp(alpha, x, mod, wup, bup, wdn, bdn, ln_g, ln_b):
    batch, seq, d = x.shape
    tm = min(MLP_TM, seq)
    assert tm % MLP_SUB == 0
    tile = pl.BlockSpec((1, tm, d), lambda b, s: (b, s, 0))
    return pl.pallas_call(
        functools.partial(_mlp_kernel, alpha),
        grid=(batch, seq // tm),
        in_specs=[tile,
                  pl.BlockSpec((1, 6, d), lambda b, s: (b, 0, 0)),
                  _resident(wup.shape), _resident(bup.shape), _resident(wdn.shape),
                  _resident(bdn.shape), _resident(ln_g.shape), _resident(ln_b.shape)],
        out_specs=tile,
        out_shape=jax.ShapeDtypeStruct(x.shape, F32),
        scratch_shapes=[pltpu.VMEM((tm // MLP_SUB, MLP_SUB, D_FF), BF16)],
        compiler_params=pltpu.CompilerParams(
            dimension_semantics=("arbitrary", "arbitrary"), vmem_limit_bytes=VMEM_LIMIT),
        name="mlp_sublayer",
    )(x, mod, wup, bup, wdn, bdn, ln_g, ln_b)


def kernel(x, c, positions, lb_logits, w_ada, b_ada, w_in, w_pa, w_pb, w_o, ln1_g, ln1_b,
           w_up, b_up, w_down, b_down, ln2_g, ln2_b):
    depth = w_in.shape[0]
    batch, seq, d = x.shape
    alpha = (2 * depth) ** 0.25
    mod = _modulation(c, w_ada, b_ada).reshape(depth, batch, 6, d)
    lbt = _lower_bound_tables(lb_logits)
    lbt = lbt.reshape(depth, 8, HG_HEADS, HG_D).transpose(0, 2, 1, 3)
    theta = (ROPE_BASE ** (-jnp.linspace(0.0, 1.0, RET_DK // 2, dtype=F32))).reshape(1, -1)
    cos_sin = _rope_table(positions, theta)
    w_blocks = _regroup_in_projection(w_in)
    for l in range(depth):
        x = _mixer(alpha, l, x, mod[l], cos_sin, lbt[l], w_blocks,
                   w_pa[l].astype(BF16).reshape(HG_PAIRS, 2 * HG_D, d),
                   w_pb[l].astype(BF16).reshape(RET_HEADS, RET_DV, d),
                   w_o[l].astype(BF16), ln1_g[l].reshape(1, d), ln1_b[l].reshape(1, d))
        x = _mlp(alpha, x, mod[l], w_up[l].astype(BF16), b_up[l].reshape(1, -1),
                 w_down[l].astype(BF16), b_down[l].reshape(1, d),
                 ln2_g[l].reshape(1, d), ln2_b[l].reshape(1, d))
    return x
```

```python
import functools

import jax
import jax.numpy as jnp
import numpy as np
from jax import lax
from jax.experimental import pallas as pl
from jax.experimental.pallas import tpu as pltpu

F32 = jnp.float32
BF16 = jnp.bfloat16

D_MODEL = 1024
HG_HEADS = 8
SUBLANES = 8
HG_D = 128
HG_PAIRS = HG_HEADS // 2
RET_HEADS = 4
RET_DK = 256
RET_DV = 512
W_BLOCK = 256
BLK_HG_Q, BLK_HG_F, BLK_HG_V, BLK_HG_G = 0, 4, 8, 12
BLK_RET_Q, BLK_RET_K, BLK_RET_V, BLK_RET_G = 16, 20, 24, 32
BLK_GATES = 40
N_BLOCKS = 48
REGROUP_BLOCKS = 4
D_FF = 4 * D_MODEL
ROPE_BASE = 10000.0
LN_EPS = 1e-5
HEAD_NORM_EPS = 1e-6
LOG2_E = 1.4426950408889634
LOCAL_ROWS = 64
LOCAL_MAX_DROP = -1.0
LEVEL_DIAGONAL, LEVEL_NONE = -2, 99

MIX_TM = 256
HG_CHUNK = 128
MLP_TM = 1024
MLP_SUB = 256
ROPE_ROWS = 1024
VMEM_LIMIT = 60 * 1024 * 1024

NT_DIMS = (((1,), (1,)), ((), ()))
TN_DIMS = (((0,), (0,)), ((), ()))


def _resident(shape):
    n = len(shape)
    return pl.BlockSpec(shape, lambda *_: (0,) * n, pipeline_mode=pl.Buffered(1))


def _layer_norm(r, g, b):
    mu = jnp.mean(r, axis=-1, keepdims=True)
    d = r - mu
    var = jnp.mean(d * d, axis=-1, keepdims=True)
    return d * lax.rsqrt(var + LN_EPS) * g + b


def _silu(z):
    return z * jax.nn.sigmoid(z)


def _mod_kernel(c_ref, w_ref, b_ref, o_ref):
    cond = _silu(c_ref[...])
    o_ref[0] = jnp.dot(cond, w_ref[0], preferred_element_type=F32,
                       precision=lax.Precision.HIGHEST) + b_ref[0]


def _modulation(c, w_ada, b_ada):
    depth, d, n = w_ada.shape
    batch = c.shape[0]
    nb = n // d
    return pl.pallas_call(
        _mod_kernel,
        grid=(depth, nb),
        in_specs=[pl.BlockSpec((batch, d), lambda l, j: (0, 0)),
                  pl.BlockSpec((1, d, d), lambda l, j: (l, 0, j)),
                  pl.BlockSpec((1, 1, d), lambda l, j: (l, 0, j))],
        out_specs=pl.BlockSpec((1, batch, d), lambda l, j: (l, 0, j)),
        out_shape=jax.ShapeDtypeStruct((depth, batch, n), F32),
        name="ada_modulation",
    )(c, w_ada, b_ada.reshape(depth, 1, n))


def _rope_kernel(pos_ref, theta_ref, o_ref):
    ang = pos_ref[0].astype(F32) * theta_ref[...]
    half = RET_DK // 2
    o_ref[0, :, 0:half] = jnp.cos(ang)
    o_ref[0, :, half:RET_DK] = jnp.sin(ang)


def _rope_table(positions, theta):
    batch, seq = positions.shape
    rows = min(ROPE_ROWS, seq)
    return pl.pallas_call(
        _rope_kernel,
        grid=(batch, seq // rows),
        in_specs=[pl.BlockSpec((1, rows, 1), lambda b, s: (b, s, 0)),
                  pl.BlockSpec(theta.shape, lambda b, s: (0, 0))],
        out_specs=pl.BlockSpec((1, rows, RET_DK), lambda b, s: (b, s, 0)),
        out_shape=jax.ShapeDtypeStruct((batch, seq, RET_DK), F32),
        name="rope_table",
    )(positions.reshape(batch, seq, 1), theta)


def _lb_kernel(depth, logit_ref, o_ref):
    z = logit_ref[...]
    e = jnp.exp(z - jnp.max(z, axis=0, keepdims=True))
    sm = e / jnp.sum(e, axis=0, keepdims=True)
    o_ref[...] = jnp.zeros_like(o_ref)
    cum = jnp.zeros_like(sm[0:1])
    for l in range(depth):
        cum = cum + sm[l:l + 1]
        lb = cum - sm[0:1]
        o_ref[l, 0:1, :] = jnp.log(lb)
        o_ref[l, 1:2, :] = jnp.log1p(-lb)
        o_ref[l, 2:3, :] = 1.0 - lb


def _lower_bound_tables(lb_logits):
    depth, n = lb_logits.shape
    return pl.pallas_call(
        functools.partial(_lb_kernel, depth),
        out_shape=jax.ShapeDtypeStruct((depth, 8, n), F32),
        name="hgrn_lower_bounds",
    )(lb_logits.astype(F32))


def _regroup_kernel(w_ref, o_ref):
    j = pl.program_id(1) * REGROUP_BLOCKS
    rotary = (j >= BLK_RET_Q) & (j < BLK_RET_V)

    @pl.when(rotary)
    def _():
        src = lax.broadcasted_iota(jnp.int32, (W_BLOCK, W_BLOCK), 0)
        dst = lax.broadcasted_iota(jnp.int32, (W_BLOCK, W_BLOCK), 1)
        half = W_BLOCK // 2
        want = jnp.where(dst < half, 2 * dst, 2 * (dst - half) + 1)
        perm = (src == want).astype(BF16)
        for t in range(REGROUP_BLOCKS):
            blk = w_ref[0, :, t * W_BLOCK:(t + 1) * W_BLOCK].astype(BF16)
            o_ref[0, t] = jnp.dot(blk, perm, preferred_element_type=F32).astype(BF16)

    @pl.when(jnp.logical_not(rotary))
    def _():
        for t in range(REGROUP_BLOCKS):
            o_ref[0, t] = w_ref[0, :, t * W_BLOCK:(t + 1) * W_BLOCK].astype(BF16)


def _regroup_in_projection(w_in):
    depth, d, n = w_in.shape
    assert n == N_BLOCKS * W_BLOCK and RET_DK == W_BLOCK
    assert BLK_RET_Q % REGROUP_BLOCKS == 0 and BLK_RET_V % REGROUP_BLOCKS == 0
    return pl.pallas_call(
        _regroup_kernel,
        grid=(depth, N_BLOCKS // REGROUP_BLOCKS),
        in_specs=[pl.BlockSpec((1, d, REGROUP_BLOCKS * W_BLOCK), lambda l, j: (l, 0, j))],
        out_specs=pl.BlockSpec((1, REGROUP_BLOCKS, d, W_BLOCK), lambda l, j: (l, j, 0, 0)),
        out_shape=jax.ShapeDtypeStruct((depth, N_BLOCKS, d, W_BLOCK), BF16),
        name="regroup_in_projection",
    )(w_in)


def _cumsum_rows(x):
    c, w = x.shape
    head_rows = lax.broadcasted_iota(jnp.int32, (SUBLANES, w), 0)
    shift = 1
    while shift < c:
        if shift < SUBLANES:
            moved = pltpu.roll(x, shift, 0)
            first = jnp.where(head_rows >= shift, moved[0:SUBLANES], 0.0)
            moved = jnp.concatenate([first, moved[SUBLANES:]], axis=0)
        else:
            moved = jnp.concatenate([jnp.zeros((shift, w), F32), x[:c - shift]], axis=0)
        x = x + moved
        shift *= 2
    return x


def _neg_abs(x):
    bits = lax.bitcast_convert_type(x, jnp.uint32) | jnp.uint32(0x80000000)
    return lax.bitcast_convert_type(bits, F32)


def _level_operands(q, k, b, b_ref, level):
    c, w = b.shape
    half = 1 << level
    if half >= SUBLANES:
        sel, expo = [], []
        for p in range(c // half):
            rows = slice(p * half, (p + 1) * half)
            mid = (p // 2) * 2 * half + half - 1
            rho = jnp.broadcast_to(b_ref[mid:mid + 1, :], (half, w))
            sel.append((q if p % 2 else k)[rows])
            expo.append(b[rows] - rho if p % 2 else rho - b[rows])
        return jnp.concatenate(sel, axis=0), jnp.exp2(jnp.concatenate(expo, axis=0))
    shape3 = (c // SUBLANES, SUBLANES, w)
    b3 = b.reshape(shape3)
    sub = lax.broadcasted_iota(jnp.int32, (1, SUBLANES, w), 1)
    rho = None
    for p in reversed(range(SUBLANES // (2 * half))):
        mid = p * 2 * half + half - 1
        piece = jnp.broadcast_to(b3[:, mid:mid + 1, :], shape3)
        rho = piece if rho is None else jnp.where(sub < (p + 1) * 2 * half, piece, rho)
    sel = jnp.where(((sub >> level) & 1) == 1, q.reshape(shape3), k.reshape(shape3))
    return sel.reshape(c, w), jnp.exp2(_neg_abs(b3 - rho)).reshape(c, w)


def _hgrn_decay_drop(b_ref):
    c = b_ref.shape[0]
    ends = [b_ref[m - 1:m, :] for m in range(LOCAL_ROWS, c + 1, LOCAL_ROWS)]
    starts = [jnp.zeros_like(ends[0])] + ends[:-1]
    return jnp.max(jnp.concatenate([s - e for s, e in zip(starts, ends)], axis=0))


def _hgrn_chunk(local_blocks, q, k, v, st_ref, lvl_ref, b_ref):
    c = q.shape[0]
    heads = (slice(0, HG_D), slice(HG_D, 2 * HG_D))
    b = b_ref[...]
    zero = jnp.zeros((c, HG_D), BF16)

    def block_diagonal(t):
        return jnp.concatenate([jnp.concatenate([t[:, heads[0]], zero], axis=1),
                                jnp.concatenate([zero, t[:, heads[1]]], axis=1)], axis=0)

    pieces = [slice(r * SUBLANES, (r + 1) * SUBLANES) for r in range(c // SUBLANES)]
    if local_blocks:
        first_level = LOCAL_ROWS.bit_length() - 1
        rho = [jnp.zeros((LOCAL_ROWS, b.shape[1]), F32)]
        rho += [jnp.broadcast_to(b_ref[m - 1:m, :], (LOCAL_ROWS, b.shape[1]))
                for m in range(LOCAL_ROWS, c, LOCAL_ROWS)]
        expo = b - jnp.concatenate(rho, axis=0)
        q_loc = (q * jnp.exp2(expo)).astype(BF16)
        k_loc = (k * jnp.exp2(-expo)).astype(BF16)
        s_local = lax.dot_general(q_loc, block_diagonal(k_loc), NT_DIMS,
                                  preferred_element_type=F32)
        scores = [jnp.where(lvl_ref[rows, :] < first_level, s_local[rows], 0.0) for rows in pieces]
    else:
        first_level = 0
        scores = [jnp.zeros((SUBLANES, 2 * c), F32)] * len(pieces)
    for level in range(first_level, c.bit_length() - 1):
        sel, dec = _level_operands(q, k, b, b_ref, level)
        z = (sel * dec).astype(BF16)
        s_level = lax.dot_general(z, block_diagonal(z), NT_DIMS, preferred_element_type=F32)
        for r, rows in enumerate(pieces):
            if rows.start >> level & 1 or (1 << level) < SUBLANES:
                scores[r] = jnp.where(lvl_ref[rows, :] == level, s_level[rows], scores[r])
    p = jnp.concatenate(scores, axis=0).astype(BF16)
    v16 = v.astype(BF16)
    o = jnp.dot(p, block_diagonal(v16), preferred_element_type=F32)
    if not local_blocks:
        qk = q * k
        diag = [jnp.broadcast_to(jnp.sum(qk[:, hd], axis=-1, keepdims=True), (c, HG_D))
                for hd in heads]
        o = o + jnp.concatenate(diag, axis=1) * v
    st = st_ref[...]
    o = o + lax.dot_general((q * jnp.exp2(b)).astype(BF16), st.astype(BF16), NT_DIMS,
                            preferred_element_type=F32)
    b_last = b_ref[c - 1:c, :]
    k_dec = (k * jnp.exp2(b_last - b)).astype(BF16)
    update = lax.dot_general(v16, k_dec, TN_DIMS, preferred_element_type=F32)
    st_dec = jnp.exp2(b_last)
    for hd in heads:
        st_ref[hd, hd] = st[hd, hd] * st_dec[:, hd] + update[hd, hd]
    return o


def _project(u, w_ref, first, count=1):
    parts = [jnp.dot(u, w_ref[first + t], preferred_element_type=F32) for t in range(count)]
    return parts[0] if count == 1 else jnp.concatenate(parts, axis=1)


def _hgrn_project(p, slot, u_ref, w_ref, lbt_ref, q_ref, k_ref, v_ref, g_ref, b_ref):
    u = u_ref[...]
    tab = jnp.concatenate([lbt_ref[2 * p], lbt_ref[2 * p + 1]], axis=1)
    log_lb, log1m_lb, one_m_lb = tab[0:1], tab[1:2], tab[2:3]
    q_ref[slot] = _silu(_project(u, w_ref, BLK_HG_Q + p))
    zf = _project(u, w_ref, BLK_HG_F + p)
    e = jnp.exp(-jnp.abs(zf))
    one_p_e = 1.0 + e
    a2 = log1m_lb + jnp.minimum(zf, 0.0) - jnp.log(one_p_e)
    lf = jnp.maximum(log_lb, a2) + jnp.log(1.0 + jnp.exp(-jnp.abs(log_lb - a2)))
    lf2 = lf * LOG2_E
    k_ref[slot] = one_m_lb * (jnp.where(zf >= 0.0, e, 1.0) / one_p_e)
    v_ref[slot] = _project(u, w_ref, BLK_HG_V + p)
    g_ref[slot] = _silu(_project(u, w_ref, BLK_HG_G + p))
    drop = None
    for ck in range(lf2.shape[0] // HG_CHUNK):
        b_ref[slot, ck] = _cumsum_rows(lf2[ck * HG_CHUNK:(ck + 1) * HG_CHUNK])
        d = _hgrn_decay_drop(b_ref.at[slot, ck])
        drop = d if drop is None else jnp.maximum(drop, d)
    return drop


def _hgrn_compute(local_blocks, p, slot, lvl_ref, q_ref, k_ref, v_ref, g_ref, b_ref, oa_ref,
                  st_ref):
    tm = q_ref.shape[1]
    for ck in range(tm // HG_CHUNK):
        rows = slice(ck * HG_CHUNK, (ck + 1) * HG_CHUNK)
        o = _hgrn_chunk(local_blocks, q_ref[slot, rows, :], k_ref[slot, rows, :],
                        v_ref[slot, rows, :], st_ref.at[p], lvl_ref, b_ref.at[slot, ck])
        for j in range(2):
            cols = slice(j * HG_D, (j + 1) * HG_D)
            oj = o[:, cols]
            oj = oj * lax.rsqrt(jnp.mean(oj * oj, axis=-1, keepdims=True) + HEAD_NORM_EPS)
            oa_ref[p, rows, cols] = (oj * g_ref[slot, rows, cols]).astype(BF16)


def _retention_project(h, slot, u_ref, w_ref, cs_ref, qdec_ref, kdec_ref,
                       q_ref, qd_ref, k_ref, kd_ref, v_ref, g_ref):
    half = RET_DK // 2
    u = u_ref[...]
    cos, sin = cs_ref[0, :, 0:half], cs_ref[0, :, half:RET_DK]

    def rotated(block, dec_ref, plain_ref, scaled_ref):
        t = _project(u, w_ref, block)
        t1, t2 = t[:, 0:half], t[:, half:RET_DK]
        r1, r2 = t1 * cos - t2 * sin, t1 * sin + t2 * cos
        dec = dec_ref[h]
        plain_ref[slot] = jnp.concatenate([r1, r2], axis=1).astype(BF16)
        scaled_ref[slot] = jnp.concatenate([r1 * dec, r2 * dec], axis=1).astype(BF16)

    rotated(BLK_RET_Q + h, qdec_ref, q_ref, qd_ref)
    rotated(BLK_RET_K + h, kdec_ref, k_ref, kd_ref)
    wide = RET_DV // W_BLOCK
    v_ref[slot] = _project(u, w_ref, BLK_RET_V + wide * h, wide).astype(BF16)
    g_ref[slot] = _silu(_project(u, w_ref, BLK_RET_G + wide * h, wide))


def _retention_compute(h, slot, dmask_ref, cdec_ref, q_ref, qd_ref, k_ref, kd_ref, v_ref, g_ref,
                       ob_ref, st_ref):
    v16 = v_ref[slot]
    scores = lax.dot_general(q_ref[slot], k_ref[slot], NT_DIMS,
                             preferred_element_type=F32) * dmask_ref[h]
    st = st_ref[h]
    o = (jnp.dot(scores.astype(BF16), v16, preferred_element_type=F32)
         + jnp.dot(qd_ref[slot], st.astype(BF16), preferred_element_type=F32))
    st_ref[h] = cdec_ref[h][0:1, :] * st + lax.dot_general(kd_ref[slot], v16, TN_DIMS,
                                                           preferred_element_type=F32)
    o = o - jnp.mean(o, axis=-1, keepdims=True)
    o = o * lax.rsqrt(jnp.mean(o * o, axis=-1, keepdims=True) + HEAD_NORM_EPS)
    ob_ref[h] = (o * g_ref[slot]).astype(BF16)


def _mixer_kernel(alpha, x_ref, mod_ref, cs_ref, lbt_ref, lvl_ref, dmask_ref, qdec_ref, kdec_ref,
                  cdec_ref, w_ref, wpa_ref, wpb_ref, wo_ref, lng_ref, lnb_ref,
                  o_ref, u_ref, hq_ref, hk_ref, hv_ref, hg_ref,
                  rq_ref, rqd_ref, rk_ref, rkd_ref, rv_ref, rg_ref,
                  oa_ref, ob_ref, sth_ref, str_ref, b_ref):
    @pl.when(pl.program_id(1) == 0)
    def _():
        sth_ref[...] = jnp.zeros_like(sth_ref)
        str_ref[...] = jnp.zeros_like(str_ref)

    x = x_ref[0]
    shift, scale, gate = mod_ref[0, 0:1, :], mod_ref[0, 1:2, :], mod_ref[0, 2:3, :]
    u_ref[...] = (x * (1.0 + scale) + shift).astype(BF16)

    assert HG_PAIRS == RET_HEADS
    hgrn_act = (hq_ref, hk_ref, hv_ref, hg_ref, b_ref)
    ret_act = (rq_ref, rqd_ref, rk_ref, rkd_ref, rv_ref, rg_ref)
    drop = None
    for i in range(HG_PAIRS):
        d = _hgrn_project(i, i, u_ref, w_ref, lbt_ref, *hgrn_act)
        drop = d if drop is None else jnp.maximum(drop, d)

    def rest(local_blocks):
        for i in range(HG_PAIRS):
            _hgrn_compute(local_blocks, i, i, lvl_ref, *hgrn_act, oa_ref, sth_ref)
            _retention_project(i, i % 2, u_ref, w_ref, cs_ref, qdec_ref, kdec_ref, *ret_act)
            _retention_compute(i, i % 2, dmask_ref, cdec_ref, *ret_act, ob_ref, str_ref)
        ya = jnp.dot(oa_ref[0], wpa_ref[0], preferred_element_type=F32)
        yb = jnp.dot(ob_ref[0], wpb_ref[0], preferred_element_type=F32)
        for i in range(1, HG_PAIRS):
            ya = ya + jnp.dot(oa_ref[i], wpa_ref[i], preferred_element_type=F32)
            yb = yb + jnp.dot(ob_ref[i], wpb_ref[i], preferred_element_type=F32)
        gates = _project(u_ref[...], w_ref, BLK_GATES, 2 * D_MODEL // W_BLOCK)
        y = (jax.nn.sigmoid(gates[:, 0:D_MODEL]) * ya
             + jax.nn.sigmoid(gates[:, D_MODEL:2 * D_MODEL]) * yb)
        y = jnp.dot(y.astype(BF16), wo_ref[...], preferred_element_type=F32)
        o_ref[0] = _layer_norm(alpha * x_ref[0] + (1.0 + gate) * y, lng_ref[...], lnb_ref[...])

    lax.cond(drop <= LOCAL_MAX_DROP, functools.partial(rest, True), functools.partial(rest, False))


def _mixer_tables(tm):
    i = np.arange(HG_CHUNK)
    differ = i[:, None] ^ i[None, :]
    level = np.where(i[:, None] > i[None, :], np.floor(np.log2(np.maximum(differ, 1))), LEVEL_NONE)
    level[i, i] = LEVEL_DIAGONAL
    log_gamma = np.log1p(-np.exp2(-5.0 - np.arange(RET_HEADS)))[:, None, None]
    r = np.arange(tm, dtype=np.float64)
    diff = r[:, None] - r[None, :]
    scale = RET_DK ** -0.5
    dmask = np.where(diff >= 0, np.exp(log_gamma * np.maximum(diff, 0.0)), 0.0) * scale
    ones = np.ones((1, 1, RET_DK // 2))
    qdec = np.exp(log_gamma * (r[None, :, None] + 1.0)) * ones
    kdec = np.exp(log_gamma * (tm - 1.0 - r[None, :, None])) * scale * ones
    cdec = np.exp(log_gamma * tm) * np.ones((1, SUBLANES, RET_DV))
    level = np.tile(level, (1, 2))
    return (jnp.asarray(level, jnp.int32), jnp.asarray(dmask, F32), jnp.asarray(qdec, F32),
            jnp.asarray(kdec, F32), jnp.asarray(cdec, F32))


def _mixer(alpha, layer, x, mod, cos_sin, lbt, w_blocks, wpa, wpb, wo, ln_g, ln_b):
    batch, seq, d = x.shape
    tm = min(MIX_TM, seq)
    tables = _mixer_tables(tm)
    tile = pl.BlockSpec((1, tm, d), lambda b, s: (b, s, 0))
    consts = (lbt,) + tables
    weights = (wpa, wpb, wo, ln_g, ln_b)
    layer_blocks = pl.BlockSpec((None,) + w_blocks.shape[1:], lambda b, s: (layer, 0, 0, 0),
                                pipeline_mode=pl.Buffered(1))
    return pl.pallas_call(
        functools.partial(_mixer_kernel, alpha),
        grid=(batch, seq // tm),
        in_specs=[tile,
                  pl.BlockSpec((1, 6, d), lambda b, s: (b, 0, 0)),
                  pl.BlockSpec((1, tm, RET_DK), lambda b, s: (b, s, 0))]
                 + [_resident(a.shape) for a in consts] + [layer_blocks]
                 + [_resident(a.shape) for a in weights],
        out_specs=tile,
        out_shape=jax.ShapeDtypeStruct(x.shape, F32),
        scratch_shapes=[pltpu.VMEM((tm, d), BF16)]
                       + [pltpu.VMEM((HG_PAIRS, tm, 2 * HG_D), F32)] * 4
                       + [pltpu.VMEM((2, tm, RET_DK), BF16)] * 4
                       + [pltpu.VMEM((2, tm, RET_DV), BF16),
                          pltpu.VMEM((2, tm, RET_DV), F32),
                          pltpu.VMEM((HG_PAIRS, tm, 2 * HG_D), BF16),
                          pltpu.VMEM((RET_HEADS, tm, RET_DV), BF16),
                          pltpu.VMEM((HG_PAIRS, 2 * HG_D, 2 * HG_D), F32),
                          pltpu.VMEM((RET_HEADS, RET_DK, RET_DV), F32),
                          pltpu.VMEM((HG_PAIRS, tm // HG_CHUNK, HG_CHUNK, 2 * HG_D), F32)],
        compiler_params=pltpu.CompilerParams(
            dimension_semantics=("arbitrary", "arbitrary"), vmem_limit_bytes=VMEM_LIMIT),
        name="mixer_sublayer",
    )(x, mod, cos_sin, *consts, w_blocks, *weights)


def _mlp_kernel(alpha, x_ref, mod_ref, wup_ref, bup_ref, wdn_ref, bdn_ref, lng_ref, lnb_ref,
                o_ref, h_ref):
    shift, scale, gate = mod_ref[0, 3:4, :], mod_ref[0, 4:5, :], mod_ref[0, 5:6, :]
    for t in range(x_ref.shape[1] // MLP_SUB):
        rows = slice(t * MLP_SUB, (t + 1) * MLP_SUB)
        x = x_ref[0, rows, :]
        u = (x * (1.0 + scale) + shift).astype(BF16)
        for j in range(D_FF // D_MODEL):
            cols = slice(j * D_MODEL, (j + 1) * D_MODEL)
            hj = jnp.dot(u, wup_ref[:, cols], preferred_element_type=F32) + bup_ref[:, cols]
            h_ref[t, :, cols] = jnp.square(jnp.maximum(hj, 0.0)).astype(BF16)
        y = jnp.dot(h_ref[t], wdn_ref[...], preferred_element_type=F32) + bdn_ref[...]
        o_ref[0, rows, :] = _layer_norm(alpha * x + (1.0 + gate) * y, lng_ref[...], lnb_ref[...])


def _mlp(alpha, x, mod, wup, bup, wdn, bdn, ln_g, ln_b):
    batch, seq, d = x.shape
    tm = min(MLP_TM, seq)
    assert tm % MLP_SUB == 0
    tile = pl.BlockSpec((1, tm, d), lambda b, s: (b, s, 0))
    return pl.pallas_call(
        functools.partial(_mlp_kernel, alpha),
        grid=(batch, seq // tm),
        in_specs=[tile,
                  pl.BlockSpec((1, 6, d), lambda b, s: (b, 0, 0)),
                  _resident(wup.shape), _resident(bup.shape), _resident(wdn.shape),
                  _resident(bdn.shape), _resident(ln_g.shape), _resident(ln_b.shape)],
        out_specs=tile,
        out_shape=jax.ShapeDtypeStruct(x.shape, F32),
        scratch_shapes=[pltpu.VMEM((tm // MLP_SUB, MLP_SUB, D_FF), BF16)],
        compiler_params=pltpu.CompilerParams(
            dimension_semantics=("arbitrary", "arbitrary"), vmem_limit_bytes=VMEM_LIMIT),
        name="mlp_sublayer",
    )(x, mod, wup, bup, wdn, bdn, ln_g, ln_b)


def kernel(x, c, positions, lb_logits, w_ada, b_ada, w_in, w_pa, w_pb, w_o, ln1_g, ln1_b,
           w_up, b_up, w_down, b_down, ln2_g, ln2_b):
    depth = w_in.shape[0]
    batch, seq, d = x.shape
    alpha = (2 * depth) ** 0.25
    mod = _modulation(c, w_ada, b_ada).reshape(depth, batch, 6, d)
    lbt = _lower_bound_tables(lb_logits)
    lbt = lbt.reshape(depth, 8, HG_HEADS, HG_D).transpose(0, 2, 1, 3)
    theta = (ROPE_BASE ** (-jnp.linspace(0.0, 1.0, RET_DK // 2, dtype=F32))).reshape(1, -1)
    cos_sin = _rope_table(positions, theta)
    w_blocks = _regroup_in_projection(w_in)
    for l in range(depth):
        x = _mixer(alpha, l, x, mod[l], cos_sin, lbt[l], w_blocks,
                   w_pa[l].astype(BF16).reshape(HG_PAIRS, 2 * HG_D, d),
                   w_pb[l].astype(BF16).reshape(RET_HEADS, RET_DV, d),
                   w_o[l].astype(BF16), ln1_g[l].reshape(1, d), ln1_b[l].reshape(1, d))
        x = _mlp(alpha, x, mod[l], w_up[l].astype(BF16), b_up[l].reshape(1, -1),
                 w_down[l].astype(BF16), b_down[l].reshape(1, d),
                 ln2_g[l].reshape(1, d), ln2_b[l].reshape(1, d))
    return x
```

```python
import functools

import jax
import jax.numpy as jnp
import numpy as np
from jax import lax
from jax.experimental import pallas as pl
from jax.experimental.pallas import tpu as pltpu

F32 = jnp.float32
BF16 = jnp.bfloat16

D_MODEL = 1024
HG_HEADS = 8
SUBLANES = 8
HG_D = 128
HG_PAIRS = HG_HEADS // 2
RET_HEADS = 4
RET_DK = 256
RET_DV = 512
W_BLOCK = 256
BLK_HG_Q, BLK_HG_F, BLK_HG_V, BLK_HG_G = 0, 4, 8, 12
BLK_RET_Q, BLK_RET_K, BLK_RET_V, BLK_RET_G = 16, 20, 24, 32
BLK_GATES = 40
N_BLOCKS = 48
REGROUP_BLOCKS = 4
D_FF = 4 * D_MODEL
ROPE_BASE = 10000.0
LN_EPS = 1e-5
HEAD_NORM_EPS = 1e-6
LOG2_E = 1.4426950408889634
LOCAL_ROWS = 64
LOCAL_MAX_DROP = 110.0
LEVEL_DIAGONAL, LEVEL_NONE = -2, 99

MIX_TM = 256
HG_CHUNK = 128
MLP_TM = 1024
MLP_SUB = 256
ROPE_ROWS = 1024
VMEM_LIMIT = 60 * 1024 * 1024

NT_DIMS = (((1,), (1,)), ((), ()))
TN_DIMS = (((0,), (0,)), ((), ()))


def _resident(shape):
    n = len(shape)
    return pl.BlockSpec(shape, lambda *_: (0,) * n, pipeline_mode=pl.Buffered(1))


def _layer_norm(r, g, b):
    mu = jnp.mean(r, axis=-1, keepdims=True)
    d = r - mu
    var = jnp.mean(d * d, axis=-1, keepdims=True)
    return d * lax.rsqrt(var + LN_EPS) * g + b


def _silu(z):
    return z * jax.nn.sigmoid(z)


def _mod_kernel(c_ref, w_ref, b_ref, o_ref):
    cond = _silu(c_ref[...])
    o_ref[0] = jnp.dot(cond, w_ref[0], preferred_element_type=F32,
                       precision=lax.Precision.HIGHEST) + b_ref[0]


def _modulation(c, w_ada, b_ada):
    depth, d, n = w_ada.shape
    batch = c.shape[0]
    nb = n // d
    return pl.pallas_call(
        _mod_kernel,
        grid=(depth, nb),
        in_specs=[pl.BlockSpec((batch, d), lambda l, j: (0, 0)),
                  pl.BlockSpec((1, d, d), lambda l, j: (l, 0, j)),
                  pl.BlockSpec((1, 1, d), lambda l, j: (l, 0, j))],
        out_specs=pl.BlockSpec((1, batch, d), lambda l, j: (l, 0, j)),
        out_shape=jax.ShapeDtypeStruct((depth, batch, n), F32),
        name="ada_modulation",
    )(c, w_ada, b_ada.reshape(depth, 1, n))


def _rope_kernel(pos_ref, theta_ref, o_ref):
    ang = pos_ref[0].astype(F32) * theta_ref[...]
    half = RET_DK // 2
    o_ref[0, :, 0:half] = jnp.cos(ang)
    o_ref[0, :, half:RET_DK] = jnp.sin(ang)


def _rope_table(positions, theta):
    batch, seq = positions.shape
    rows = min(ROPE_ROWS, seq)
    return pl.pallas_call(
        _rope_kernel,
        grid=(batch, seq // rows),
        in_specs=[pl.BlockSpec((1, rows, 1), lambda b, s: (b, s, 0)),
                  pl.BlockSpec(theta.shape, lambda b, s: (0, 0))],
        out_specs=pl.BlockSpec((1, rows, RET_DK), lambda b, s: (b, s, 0)),
        out_shape=jax.ShapeDtypeStruct((batch, seq, RET_DK), F32),
        name="rope_table",
    )(positions.reshape(batch, seq, 1), theta)


def _lb_kernel(depth, logit_ref, o_ref):
    z = logit_ref[...]
    e = jnp.exp(z - jnp.max(z, axis=0, keepdims=True))
    sm = e / jnp.sum(e, axis=0, keepdims=True)
    o_ref[...] = jnp.zeros_like(o_ref)
    cum = jnp.zeros_like(sm[0:1])
    for l in range(depth):
        cum = cum + sm[l:l + 1]
        lb = cum - sm[0:1]
        o_ref[l, 0:1, :] = jnp.log(lb)
        o_ref[l, 1:2, :] = jnp.log1p(-lb)
        o_ref[l, 2:3, :] = 1.0 - lb


def _lower_bound_tables(lb_logits):
    depth, n = lb_logits.shape
    return pl.pallas_call(
        functools.partial(_lb_kernel, depth),
        out_shape=jax.ShapeDtypeStruct((depth, 8, n), F32),
        name="hgrn_lower_bounds",
    )(lb_logits.astype(F32))


def _regroup_kernel(w_ref, o_ref):
    j = pl.program_id(1) * REGROUP_BLOCKS
    rotary = (j >= BLK_RET_Q) & (j < BLK_RET_V)

    @pl.when(rotary)
    def _():
        src = lax.broadcasted_iota(jnp.int32, (W_BLOCK, W_BLOCK), 0)
        dst = lax.broadcasted_iota(jnp.int32, (W_BLOCK, W_BLOCK), 1)
        half = W_BLOCK // 2
        want = jnp.where(dst < half, 2 * dst, 2 * (dst - half) + 1)
        perm = (src == want).astype(BF16)
        for t in range(REGROUP_BLOCKS):
            blk = w_ref[0, :, t * W_BLOCK:(t + 1) * W_BLOCK].astype(BF16)
            o_ref[0, t] = jnp.dot(blk, perm, preferred_element_type=F32).astype(BF16)

    @pl.when(jnp.logical_not(rotary))
    def _():
        for t in range(REGROUP_BLOCKS):
            o_ref[0, t] = w_ref[0, :, t * W_BLOCK:(t + 1) * W_BLOCK].astype(BF16)


def _regroup_in_projection(w_in):
    depth, d, n = w_in.shape
    assert n == N_BLOCKS * W_BLOCK and RET_DK == W_BLOCK
    assert BLK_RET_Q % REGROUP_BLOCKS == 0 and BLK_RET_V % REGROUP_BLOCKS == 0
    return pl.pallas_call(
        _regroup_kernel,
        grid=(depth, N_BLOCKS // REGROUP_BLOCKS),
        in_specs=[pl.BlockSpec((1, d, REGROUP_BLOCKS * W_BLOCK), lambda l, j: (l, 0, j))],
        out_specs=pl.BlockSpec((1, REGROUP_BLOCKS, d, W_BLOCK), lambda l, j: (l, j, 0, 0)),
        out_shape=jax.ShapeDtypeStruct((depth, N_BLOCKS, d, W_BLOCK), BF16),
        name="regroup_in_projection",
    )(w_in)


def _cumsum_rows(x):
    c, w = x.shape
    head_rows = lax.broadcasted_iota(jnp.int32, (SUBLANES, w), 0)
    shift = 1
    while shift < c:
        if shift < SUBLANES:
            moved = pltpu.roll(x, shift, 0)
            first = jnp.where(head_rows >= shift, moved[0:SUBLANES], 0.0)
            moved = jnp.concatenate([first, moved[SUBLANES:]], axis=0)
        else:
            moved = jnp.concatenate([jnp.zeros((shift, w), F32), x[:c - shift]], axis=0)
        x = x + moved
        shift *= 2
    return x


def _neg_abs(x):
    bits = lax.bitcast_convert_type(x, jnp.uint32) | jnp.uint32(0x80000000)
    return lax.bitcast_convert_type(bits, F32)


def _level_operands(q, k, b, b_ref, level):
    c, w = b.shape
    half = 1 << level
    if half >= SUBLANES:
        sel, expo = [], []
        for p in range(c // half):
            rows = slice(p * half, (p + 1) * half)
            mid = (p // 2) * 2 * half + half - 1
            rho = jnp.broadcast_to(b_ref[mid:mid + 1, :], (half, w))
            sel.append((q if p % 2 else k)[rows])
            expo.append(b[rows] - rho if p % 2 else rho - b[rows])
        return jnp.concatenate(sel, axis=0), jnp.exp2(jnp.concatenate(expo, axis=0))
    shape3 = (c // SUBLANES, SUBLANES, w)
    b3 = b.reshape(shape3)
    sub = lax.broadcasted_iota(jnp.int32, (1, SUBLANES, w), 1)
    rho = None
    for p in reversed(range(SUBLANES // (2 * half))):
        mid = p * 2 * half + half - 1
        piece = jnp.broadcast_to(b3[:, mid:mid + 1, :], shape3)
        rho = piece if rho is None else jnp.where(sub < (p + 1) * 2 * half, piece, rho)
    sel = jnp.where(((sub >> level) & 1) == 1, q.reshape(shape3), k.reshape(shape3))
    return sel.reshape(c, w), jnp.exp2(_neg_abs(b3 - rho)).reshape(c, w)


def _hgrn_decay_drop(b_ref):
    c = b_ref.shape[0]
    ends = [b_ref[m - 1:m, :] for m in range(LOCAL_ROWS, c + 1, LOCAL_ROWS)]
    starts = [jnp.zeros_like(ends[0])] + ends[:-1]
    return jnp.max(jnp.concatenate([s - e for s, e in zip(starts, ends)], axis=0))


def _hgrn_chunk(local_blocks, q, k, v, st_ref, lvl_ref, b_ref):
    c = q.shape[0]
    heads = (slice(0, HG_D), slice(HG_D, 2 * HG_D))
    b = b_ref[...]
    zero = jnp.zeros((c, HG_D), BF16)

    def block_diagonal(t):
        return jnp.concatenate([jnp.concatenate([t[:, heads[0]], zero], axis=1),
                                jnp.concatenate([zero, t[:, heads[1]]], axis=1)], axis=0)

    pieces = [slice(r * SUBLANES, (r + 1) * SUBLANES) for r in range(c // SUBLANES)]
    if local_blocks:
        span = 2 * LOCAL_ROWS
        first_level = span.bit_length() - 1
        rho = [jnp.broadcast_to(b_ref[m + LOCAL_ROWS - 1:m + LOCAL_ROWS, :], (span, b.shape[1]))
               for m in range(0, c, span)]
        expo = b - jnp.concatenate(rho, axis=0)
        q_loc = (q * jnp.exp2(expo)).astype(BF16)
        k_loc = (k * jnp.exp2(-expo)).astype(BF16)
        s_local = lax.dot_general(q_loc, block_diagonal(k_loc), NT_DIMS,
                                  preferred_element_type=F32)
        scores = [jnp.where(lvl_ref[rows, :] < first_level, s_local[rows], 0.0) for rows in pieces]
    else:
        first_level = 0
        scores = [jnp.zeros((SUBLANES, 2 * c), F32)] * len(pieces)
    for level in range(first_level, c.bit_length() - 1):
        sel, dec = _level_operands(q, k, b, b_ref, level)
        z = (sel * dec).astype(BF16)
        s_level = lax.dot_general(z, block_diagonal(z), NT_DIMS, preferred_element_type=F32)
        for r, rows in enumerate(pieces):
            if rows.start >> level & 1 or (1 << level) < SUBLANES:
                scores[r] = jnp.where(lvl_ref[rows, :] == level, s_level[rows], scores[r])
    p = jnp.concatenate(scores, axis=0).astype(BF16)
    v16 = v.astype(BF16)
    o = jnp.dot(p, block_diagonal(v16), preferred_element_type=F32)
    if not local_blocks:
        qk = q * k
        diag = [jnp.broadcast_to(jnp.sum(qk[:, hd], axis=-1, keepdims=True), (c, HG_D))
                for hd in heads]
        o = o + jnp.concatenate(diag, axis=1) * v
    st = st_ref[...]
    o = o + lax.dot_general((q * jnp.exp2(b)).astype(BF16), st.astype(BF16), NT_DIMS,
                            preferred_element_type=F32)
    b_last = b_ref[c - 1:c, :]
    k_dec = (k * jnp.exp2(b_last - b)).astype(BF16)
    update = lax.dot_general(v16, k_dec, TN_DIMS, preferred_element_type=F32)
    st_dec = jnp.exp2(b_last)
    for hd in heads:
        st_ref[hd, hd] = st[hd, hd] * st_dec[:, hd] + update[hd, hd]
    return o


def _project(u, w_ref, first, count=1):
    parts = [jnp.dot(u, w_ref[first + t], preferred_element_type=F32) for t in range(count)]
    return parts[0] if count == 1 else jnp.concatenate(parts, axis=1)


def _hgrn_project(p, slot, u_ref, w_ref, lbt_ref, q_ref, k_ref, v_ref, g_ref, b_ref):
    u = u_ref[...]
    tab = jnp.concatenate([lbt_ref[2 * p], lbt_ref[2 * p + 1]], axis=1)
    log_lb, log1m_lb, one_m_lb = tab[0:1], tab[1:2], tab[2:3]
    q_ref[slot] = _silu(_project(u, w_ref, BLK_HG_Q + p))
    zf = _project(u, w_ref, BLK_HG_F + p)
    e = jnp.exp(-jnp.abs(zf))
    one_p_e = 1.0 + e
    a2 = log1m_lb + jnp.minimum(zf, 0.0) - jnp.log(one_p_e)
    lf = jnp.maximum(log_lb, a2) + jnp.log(1.0 + jnp.exp(-jnp.abs(log_lb - a2)))
    lf2 = lf * LOG2_E
    k_ref[slot] = one_m_lb * (jnp.where(zf >= 0.0, e, 1.0) / one_p_e)
    v_ref[slot] = _project(u, w_ref, BLK_HG_V + p)
    g_ref[slot] = _silu(_project(u, w_ref, BLK_HG_G + p))
    drop = None
    for ck in range(lf2.shape[0] // HG_CHUNK):
        b_ref[slot, ck] = _cumsum_rows(lf2[ck * HG_CHUNK:(ck + 1) * HG_CHUNK])
        d = _hgrn_decay_drop(b_ref.at[slot, ck])
        drop = d if drop is None else jnp.maximum(drop, d)
    return drop


def _hgrn_compute(local_blocks, p, slot, lvl_ref, q_ref, k_ref, v_ref, g_ref, b_ref, oa_ref,
                  st_ref):
    tm = q_ref.shape[1]
    for ck in range(tm // HG_CHUNK):
        rows = slice(ck * HG_CHUNK, (ck + 1) * HG_CHUNK)
        o = _hgrn_chunk(local_blocks, q_ref[slot, rows, :], k_ref[slot, rows, :],
                        v_ref[slot, rows, :], st_ref.at[p], lvl_ref, b_ref.at[slot, ck])
        for j in range(2):
            cols = slice(j * HG_D, (j + 1) * HG_D)
            oj = o[:, cols]
            oj = oj * lax.rsqrt(jnp.mean(oj * oj, axis=-1, keepdims=True) + HEAD_NORM_EPS)
            oa_ref[p, rows, cols] = (oj * g_ref[slot, rows, cols]).astype(BF16)


def _retention_project(h, slot, u_ref, w_ref, cs_ref, qdec_ref, kdec_ref,
                       q_ref, qd_ref, k_ref, kd_ref, v_ref, g_ref):
    half = RET_DK // 2
    u = u_ref[...]
    cos, sin = cs_ref[0, :, 0:half], cs_ref[0, :, half:RET_DK]

    def rotated(block, dec_ref, plain_ref, scaled_ref):
        t = _project(u, w_ref, block)
        t1, t2 = t[:, 0:half], t[:, half:RET_DK]
        r1, r2 = t1 * cos - t2 * sin, t1 * sin + t2 * cos
        dec = dec_ref[h]
        plain_ref[slot] = jnp.concatenate([r1, r2], axis=1).astype(BF16)
        scaled_ref[slot] = jnp.concatenate([r1 * dec, r2 * dec], axis=1).astype(BF16)

    rotated(BLK_RET_Q + h, qdec_ref, q_ref, qd_ref)
    rotated(BLK_RET_K + h, kdec_ref, k_ref, kd_ref)
    wide = RET_DV // W_BLOCK
    v_ref[slot] = _project(u, w_ref, BLK_RET_V + wide * h, wide).astype(BF16)
    g_ref[slot] = _silu(_project(u, w_ref, BLK_RET_G + wide * h, wide))


def _retention_compute(h, slot, dmask_ref, cdec_ref, q_ref, qd_ref, k_ref, kd_ref, v_ref, g_ref,
                       ob_ref, st_ref):
    v16 = v_ref[slot]
    scores = lax.dot_general(q_ref[slot], k_ref[slot], NT_DIMS,
                             preferred_element_type=F32) * dmask_ref[h]
    st = st_ref[h]
    o = (jnp.dot(scores.astype(BF16), v16, preferred_element_type=F32)
         + jnp.dot(qd_ref[slot], st.astype(BF16), preferred_element_type=F32))
    st_ref[h] = cdec_ref[h][0:1, :] * st + lax.dot_general(kd_ref[slot], v16, TN_DIMS,
                                                           preferred_element_type=F32)
    o = o - jnp.mean(o, axis=-1, keepdims=True)
    o = o * lax.rsqrt(jnp.mean(o * o, axis=-1, keepdims=True) + HEAD_NORM_EPS)
    ob_ref[h] = (o * g_ref[slot]).astype(BF16)


def _mixer_kernel(alpha, x_ref, mod_ref, cs_ref, lbt_ref, lvl_ref, dmask_ref, qdec_ref, kdec_ref,
                  cdec_ref, w_ref, wpa_ref, wpb_ref, wo_ref, lng_ref, lnb_ref,
                  o_ref, u_ref, hq_ref, hk_ref, hv_ref, hg_ref,
                  rq_ref, rqd_ref, rk_ref, rkd_ref, rv_ref, rg_ref,
                  oa_ref, ob_ref, sth_ref, str_ref, b_ref):
    @pl.when(pl.program_id(1) == 0)
    def _():
        sth_ref[...] = jnp.zeros_like(sth_ref)
        str_ref[...] = jnp.zeros_like(str_ref)

    x = x_ref[0]
    shift, scale, gate = mod_ref[0, 0:1, :], mod_ref[0, 1:2, :], mod_ref[0, 2:3, :]
    u_ref[...] = (x * (1.0 + scale) + shift).astype(BF16)

    assert HG_PAIRS == RET_HEADS
    hgrn_act = (hq_ref, hk_ref, hv_ref, hg_ref, b_ref)
    ret_act = (rq_ref, rqd_ref, rk_ref, rkd_ref, rv_ref, rg_ref)
    drop = None
    for i in range(HG_PAIRS):
        d = _hgrn_project(i, i, u_ref, w_ref, lbt_ref, *hgrn_act)
        drop = d if drop is None else jnp.maximum(drop, d)

    def rest(local_blocks):
        for i in range(HG_PAIRS):
            _hgrn_compute(local_blocks, i, i, lvl_ref, *hgrn_act, oa_ref, sth_ref)
            _retention_project(i, i % 2, u_ref, w_ref, cs_ref, qdec_ref, kdec_ref, *ret_act)
            _retention_compute(i, i % 2, dmask_ref, cdec_ref, *ret_act, ob_ref, str_ref)
        ya = jnp.dot(oa_ref[0], wpa_ref[0], preferred_element_type=F32)
        yb = jnp.dot(ob_ref[0], wpb_ref[0], preferred_element_type=F32)
        for i in range(1, HG_PAIRS):
            ya = ya + jnp.dot(oa_ref[i], wpa_ref[i], preferred_element_type=F32)
            yb = yb + jnp.dot(ob_ref[i], wpb_ref[i], preferred_element_type=F32)
        gates = _project(u_ref[...], w_ref, BLK_GATES, 2 * D_MODEL // W_BLOCK)
        y = (jax.nn.sigmoid(gates[:, 0:D_MODEL]) * ya
             + jax.nn.sigmoid(gates[:, D_MODEL:2 * D_MODEL]) * yb)
        y = jnp.dot(y.astype(BF16), wo_ref[...], preferred_element_type=F32)
        o_ref[0] = _layer_norm(alpha * x_ref[0] + (1.0 + gate) * y, lng_ref[...], lnb_ref[...])

    lax.cond(drop <= LOCAL_MAX_DROP, functools.partial(rest, True), functools.partial(rest, False))


def _mixer_tables(tm):
    i = np.arange(HG_CHUNK)
    differ = i[:, None] ^ i[None, :]
    level = np.where(i[:, None] > i[None, :], np.floor(np.log2(np.maximum(differ, 1))), LEVEL_NONE)
    level[i, i] = LEVEL_DIAGONAL
    log_gamma = np.log1p(-np.exp2(-5.0 - np.arange(RET_HEADS)))[:, None, None]
    r = np.arange(tm, dtype=np.float64)
    diff = r[:, None] - r[None, :]
    scale = RET_DK ** -0.5
    dmask = np.where(diff >= 0, np.exp(log_gamma * np.maximum(diff, 0.0)), 0.0) * scale
    ones = np.ones((1, 1, RET_DK // 2))
    qdec = np.exp(log_gamma * (r[None, :, None] + 1.0)) * ones
    kdec = np.exp(log_gamma * (tm - 1.0 - r[None, :, None])) * scale * ones
    cdec = np.exp(log_gamma * tm) * np.ones((1, SUBLANES, RET_DV))
    level = np.tile(level, (1, 2))
    return (jnp.asarray(level, jnp.int32), jnp.asarray(dmask, F32), jnp.asarray(qdec, F32),
            jnp.asarray(kdec, F32), jnp.asarray(cdec, F32))


def _mixer(alpha, layer, x, mod, cos_sin, lbt, w_blocks, wpa, wpb, wo, ln_g, ln_b):
    batch, seq, d = x.shape
    tm = min(MIX_TM, seq)
    tables = _mixer_tables(tm)
    tile = pl.BlockSpec((1, tm, d), lambda b, s: (b, s, 0))
    consts = (lbt,) + tables
    weights = (wpa, wpb, wo, ln_g, ln_b)
    layer_blocks = pl.BlockSpec((None,) + w_blocks.shape[1:], lambda b, s: (layer, 0, 0, 0),
                                pipeline_mode=pl.Buffered(1))
    return pl.pallas_call(
        functools.partial(_mixer_kernel, alpha),
        grid=(batch, seq // tm),
        in_specs=[tile,
                  pl.BlockSpec((1, 6, d), lambda b, s: (b, 0, 0)),
                  pl.BlockSpec((1, tm, RET_DK), lambda b, s: (b, s, 0))]
                 + [_resident(a.shape) for a in consts] + [layer_blocks]
                 + [_resident(a.shape) for a in weights],
        out_specs=tile,
        out_shape=jax.ShapeDtypeStruct(x.shape, F32),
        scratch_shapes=[pltpu.VMEM((tm, d), BF16)]
                       + [pltpu.VMEM((HG_PAIRS, tm, 2 * HG_D), F32)] * 4
                       + [pltpu.VMEM((2, tm, RET_DK), BF16)] * 4
                       + [pltpu.VMEM((2, tm, RET_DV), BF16),
                          pltpu.VMEM((2, tm, RET_DV), F32),
                          pltpu.VMEM((HG_PAIRS, tm, 2 * HG_D), BF16),
                          pltpu.VMEM((RET_HEADS, tm, RET_DV), BF16),
                          pltpu.VMEM((HG_PAIRS, 2 * HG_D, 2 * HG_D), F32),
                          pltpu.VMEM((RET_HEADS, RET_DK, RET_DV), F32),
                          pltpu.VMEM((HG_PAIRS, tm // HG_CHUNK, HG_CHUNK, 2 * HG_D), F32)],
        compiler_params=pltpu.CompilerParams(
            dimension_semantics=("arbitrary", "arbitrary"), vmem_limit_bytes=VMEM_LIMIT),
        name="mixer_sublayer",
    )(x, mod, cos_sin, *consts, w_blocks, *weights)


def _mlp_kernel(alpha, x_ref, mod_ref, wup_ref, bup_ref, wdn_ref, bdn_ref, lng_ref, lnb_ref,
                o_ref, h_ref):
    shift, scale, gate = mod_ref[0, 3:4, :], mod_ref[0, 4:5, :], mod_ref[0, 5:6, :]
    for t in range(x_ref.shape[1] // MLP_SUB):
        rows = slice(t * MLP_SUB, (t + 1) * MLP_SUB)
        x = x_ref[0, rows, :]
        u = (x * (1.0 + scale) + shift).astype(BF16)
        for j in range(D_FF // D_MODEL):
            cols = slice(j * D_MODEL, (j + 1) * D_MODEL)
            hj = jnp.dot(u, wup_ref[:, cols], preferred_element_type=F32) + bup_ref[:, cols]
            h_ref[t, :, cols] = jnp.square(jnp.maximum(hj, 0.0)).astype(BF16)
        y = jnp.dot(h_ref[t], wdn_ref[...], preferred_element_type=F32) + bdn_ref[...]
        o_ref[0, rows, :] = _layer_norm(alpha * x + (1.0 + gate) * y, lng_ref[...], lnb_ref[...])


def _mlp(alpha, x, mod, wup, bup, wdn, bdn, ln_g, ln_b):
    batch, seq, d = x.shape
    tm = min(MLP_TM, seq)
    assert tm % MLP_SUB == 0
    tile = pl.BlockSpec((1, tm, d), lambda b, s: (b, s, 0))
    return pl.pallas_call(
        functools.partial(_mlp_kernel, alpha),
        grid=(batch, seq // tm),
        in_specs=[tile,
                  pl.BlockSpec((1, 6, d), lambda b, s: (b, 0, 0)),
                  _resident(wup.shape), _resident(bup.shape), _resident(wdn.shape),
                  _resident(bdn.shape), _resident(ln_g.shape), _resident(ln_b.shape)],
        out_specs=tile,
        out_shape=jax.ShapeDtypeStruct(x.shape, F32),
        scratch_shapes=[pltpu.VMEM((tm // MLP_SUB, MLP_SUB, D_FF), BF16)],
        compiler_params=pltpu.CompilerParams(
            dimension_semantics=("arbitrary", "arbitrary"), vmem_limit_bytes=VMEM_LIMIT),
        name="mlp_sublayer",
    )(x, mod, wup, bup, wdn, bdn, ln_g, ln_b)


def kernel(x, c, positions, lb_logits, w_ada, b_ada, w_in, w_pa, w_pb, w_o, ln1_g, ln1_b,
           w_up, b_up, w_down, b_down, ln2_g, ln2_b):
    depth = w_in.shape[0]
    batch, seq, d = x.shape
    alpha = (2 * depth) ** 0.25
    mod = _modulation(c, w_ada, b_ada).reshape(depth, batch, 6, d)
    lbt = _lower_bound_tables(lb_logits)
    lbt = lbt.reshape(depth, 8, HG_HEADS, HG_D).transpose(0, 2, 1, 3)
    theta = (ROPE_BASE ** (-jnp.linspace(0.0, 1.0, RET_DK // 2, dtype=F32))).reshape(1, -1)
    cos_sin = _rope_table(positions, theta)
    w_blocks = _regroup_in_projection(w_in)
    for l in range(depth):
        x = _mixer(alpha, l, x, mod[l], cos_sin, lbt[l], w_blocks,
                   w_pa[l].astype(BF16).reshape(HG_PAIRS, 2 * HG_D, d),
                   w_pb[l].astype(BF16).reshape(RET_HEADS, RET_DV, d),
                   w_o[l].astype(BF16), ln1_g[l].reshape(1, d), ln1_b[l].reshape(1, d))
        x = _mlp(alpha, x, mod[l], w_up[l].astype(BF16), b_up[l].reshape(1, -1),
                 w_down[l].astype(BF16), b_down[l].reshape(1, d),
                 ln2_g[l].reshape(1, d), ln2_b[l].reshape(1, d))
    return x
```

```python
import functools

import jax
import jax.numpy as jnp
import numpy as np
from jax import lax
from jax.experimental import pallas as pl
from jax.experimental.pallas import tpu as pltpu

F32 = jnp.float32
BF16 = jnp.bfloat16

D_MODEL = 1024
HG_HEADS = 8
SUBLANES = 8
HG_D = 128
HG_PAIRS = HG_HEADS // 2
RET_HEADS = 4
RET_DK = 256
RET_DV = 512
W_BLOCK = 256
BLK_HG_Q, BLK_HG_F, BLK_HG_V, BLK_HG_G = 0, 4, 8, 12
BLK_RET_Q, BLK_RET_K, BLK_RET_V, BLK_RET_G = 16, 20, 24, 32
BLK_GATES = 40
N_BLOCKS = 48
REGROUP_BLOCKS = 4
D_FF = 4 * D_MODEL
ROPE_BASE = 10000.0
LN_EPS = 1e-5
HEAD_NORM_EPS = 1e-6
LOG2_E = 1.4426950408889634
LOCAL_ROWS = 64
LOCAL_MAX_DROP = 110.0
LEVEL_DIAGONAL, LEVEL_NONE = -2, 99

MIX_TM = 256
HG_CHUNK = 128
MLP_TM = 1024
MLP_SUB = 256
ROPE_ROWS = 1024
VMEM_LIMIT = 60 * 1024 * 1024

NT_DIMS = (((1,), (1,)), ((), ()))
TN_DIMS = (((0,), (0,)), ((), ()))


def _resident(shape):
    n = len(shape)
    return pl.BlockSpec(shape, lambda *_: (0,) * n, pipeline_mode=pl.Buffered(1))


def _layer_norm(r, g, b):
    mu = jnp.mean(r, axis=-1, keepdims=True)
    d = r - mu
    var = jnp.mean(d * d, axis=-1, keepdims=True)
    return d * lax.rsqrt(var + LN_EPS) * g + b


def _silu(z):
    return z * jax.nn.sigmoid(z)


def _mod_kernel(c_ref, w_ref, b_ref, o_ref):
    cond = _silu(c_ref[...])
    o_ref[0] = jnp.dot(cond, w_ref[0], preferred_element_type=F32,
                       precision=lax.Precision.HIGHEST) + b_ref[0]


def _modulation(c, w_ada, b_ada):
    depth, d, n = w_ada.shape
    batch = c.shape[0]
    nb = n // d
    return pl.pallas_call(
        _mod_kernel,
        grid=(depth, nb),
        in_specs=[pl.BlockSpec((batch, d), lambda l, j: (0, 0)),
                  pl.BlockSpec((1, d, d), lambda l, j: (l, 0, j)),
                  pl.BlockSpec((1, 1, d), lambda l, j: (l, 0, j))],
        out_specs=pl.BlockSpec((1, batch, d), lambda l, j: (l, 0, j)),
        out_shape=jax.ShapeDtypeStruct((depth, batch, n), F32),
        name="ada_modulation",
    )(c, w_ada, b_ada.reshape(depth, 1, n))


def _rope_kernel(pos_ref, theta_ref, o_ref):
    ang = pos_ref[0].astype(F32) * theta_ref[...]
    half = RET_DK // 2
    o_ref[0, :, 0:half] = jnp.cos(ang)
    o_ref[0, :, half:RET_DK] = jnp.sin(ang)


def _rope_table(positions, theta):
    batch, seq = positions.shape
    rows = min(ROPE_ROWS, seq)
    return pl.pallas_call(
        _rope_kernel,
        grid=(batch, seq // rows),
        in_specs=[pl.BlockSpec((1, rows, 1), lambda b, s: (b, s, 0)),
                  pl.BlockSpec(theta.shape, lambda b, s: (0, 0))],
        out_specs=pl.BlockSpec((1, rows, RET_DK), lambda b, s: (b, s, 0)),
        out_shape=jax.ShapeDtypeStruct((batch, seq, RET_DK), F32),
        name="rope_table",
    )(positions.reshape(batch, seq, 1), theta)


def _lb_kernel(depth, logit_ref, o_ref):
    z = logit_ref[...]
    e = jnp.exp(z - jnp.max(z, axis=0, keepdims=True))
    sm = e / jnp.sum(e, axis=0, keepdims=True)
    o_ref[...] = jnp.zeros_like(o_ref)
    cum = jnp.zeros_like(sm[0:1])
    for l in range(depth):
        cum = cum + sm[l:l + 1]
        lb = cum - sm[0:1]
        o_ref[l, 0:1, :] = jnp.log(lb)
        o_ref[l, 1:2, :] = jnp.log1p(-lb)
        o_ref[l, 2:3, :] = 1.0 - lb


def _lower_bound_tables(lb_logits):
    depth, n = lb_logits.shape
    return pl.pallas_call(
        functools.partial(_lb_kernel, depth),
        out_shape=jax.ShapeDtypeStruct((depth, 8, n), F32),
        name="hgrn_lower_bounds",
    )(lb_logits.astype(F32))


def _regroup_kernel(w_ref, o_ref):
    j = pl.program_id(1) * REGROUP_BLOCKS
    rotary = (j >= BLK_RET_Q) & (j < BLK_RET_V)

    @pl.when(rotary)
    def _():
        src = lax.broadcasted_iota(jnp.int32, (W_BLOCK, W_BLOCK), 0)
        dst = lax.broadcasted_iota(jnp.int32, (W_BLOCK, W_BLOCK), 1)
        half = W_BLOCK // 2
        want = jnp.where(dst < half, 2 * dst, 2 * (dst - half) + 1)
        perm = (src == want).astype(BF16)
        for t in range(REGROUP_BLOCKS):
            blk = w_ref[0, :, t * W_BLOCK:(t + 1) * W_BLOCK].astype(BF16)
            o_ref[0, t] = jnp.dot(blk, perm, preferred_element_type=F32).astype(BF16)

    @pl.when(jnp.logical_not(rotary))
    def _():
        for t in range(REGROUP_BLOCKS):
            o_ref[0, t] = w_ref[0, :, t * W_BLOCK:(t + 1) * W_BLOCK].astype(BF16)


def _regroup_in_projection(w_in):
    depth, d, n = w_in.shape
    assert n == N_BLOCKS * W_BLOCK and RET_DK == W_BLOCK
    assert BLK_RET_Q % REGROUP_BLOCKS == 0 and BLK_RET_V % REGROUP_BLOCKS == 0
    return pl.pallas_call(
        _regroup_kernel,
        grid=(depth, N_BLOCKS // REGROUP_BLOCKS),
        in_specs=[pl.BlockSpec((1, d, REGROUP_BLOCKS * W_BLOCK), lambda l, j: (l, 0, j))],
        out_specs=pl.BlockSpec((1, REGROUP_BLOCKS, d, W_BLOCK), lambda l, j: (l, j, 0, 0)),
        out_shape=jax.ShapeDtypeStruct((depth, N_BLOCKS, d, W_BLOCK), BF16),
        name="regroup_in_projection",
    )(w_in)


def _cumsum_rows(x):
    c, w = x.shape
    head_rows = lax.broadcasted_iota(jnp.int32, (SUBLANES, w), 0)
    shift = 1
    while shift < c:
        if shift < SUBLANES:
            moved = pltpu.roll(x, shift, 0)
            first = jnp.where(head_rows >= shift, moved[0:SUBLANES], 0.0)
            moved = jnp.concatenate([first, moved[SUBLANES:]], axis=0)
        else:
            moved = jnp.concatenate([jnp.zeros((shift, w), F32), x[:c - shift]], axis=0)
        x = x + moved
        shift *= 2
    return x


def _neg_abs(x):
    bits = lax.bitcast_convert_type(x, jnp.uint32) | jnp.uint32(0x80000000)
    return lax.bitcast_convert_type(bits, F32)


def _level_operands(q, k, b, b_ref, level):
    c, w = b.shape
    half = 1 << level
    if half >= SUBLANES:
        sel, expo = [], []
        for p in range(c // half):
            rows = slice(p * half, (p + 1) * half)
            mid = (p // 2) * 2 * half + half - 1
            rho = jnp.broadcast_to(b_ref[mid:mid + 1, :], (half, w))
            sel.append((q if p % 2 else k)[rows])
            expo.append(b[rows] - rho if p % 2 else rho - b[rows])
        return jnp.concatenate(sel, axis=0), jnp.exp2(jnp.concatenate(expo, axis=0))
    shape3 = (c // SUBLANES, SUBLANES, w)
    b3 = b.reshape(shape3)
    sub = lax.broadcasted_iota(jnp.int32, (1, SUBLANES, w), 1)
    rho = None
    for p in reversed(range(SUBLANES // (2 * half))):
        mid = p * 2 * half + half - 1
        piece = jnp.broadcast_to(b3[:, mid:mid + 1, :], shape3)
        rho = piece if rho is None else jnp.where(sub < (p + 1) * 2 * half, piece, rho)
    sel = jnp.where(((sub >> level) & 1) == 1, q.reshape(shape3), k.reshape(shape3))
    return sel.reshape(c, w), jnp.exp2(_neg_abs(b3 - rho)).reshape(c, w)


def _hgrn_decay_drop(b_ref):
    c = b_ref.shape[0]
    ends = [b_ref[m - 1:m, :] for m in range(LOCAL_ROWS, c + 1, LOCAL_ROWS)]
    starts = [jnp.zeros_like(ends[0])] + ends[:-1]
    return jnp.max(jnp.concatenate([s - e for s, e in zip(starts, ends)], axis=0))


def _hgrn_chunk(local_blocks, q, k, v, st_ref, lvl_ref, b_ref):
    c = q.shape[0]
    heads = (slice(0, HG_D), slice(HG_D, 2 * HG_D))
    b = b_ref[...]
    zero = jnp.zeros((c, HG_D), BF16)

    def block_diagonal(t):
        return jnp.concatenate([jnp.concatenate([t[:, heads[0]], zero], axis=1),
                                jnp.concatenate([zero, t[:, heads[1]]], axis=1)], axis=0)

    pieces = [slice(r * SUBLANES, (r + 1) * SUBLANES) for r in range(c // SUBLANES)]
    if local_blocks:
        span = 2 * LOCAL_ROWS
        first_level = span.bit_length() - 1
        rho = [jnp.broadcast_to(b_ref[m + LOCAL_ROWS - 1:m + LOCAL_ROWS, :], (span, b.shape[1]))
               for m in range(0, c, span)]
        expo = b - jnp.concatenate(rho, axis=0)
        q_loc = (q * jnp.exp2(expo)).astype(BF16)
        k_loc = (k * jnp.exp2(-expo)).astype(BF16)
        s_local = lax.dot_general(q_loc, block_diagonal(k_loc), NT_DIMS,
                                  preferred_element_type=F32)
        scores = [jnp.where(lvl_ref[rows, :] < first_level, s_local[rows], 0.0) for rows in pieces]
    else:
        first_level = 0
        scores = [jnp.zeros((SUBLANES, 2 * c), F32)] * len(pieces)
    for level in range(first_level, c.bit_length() - 1):
        sel, dec = _level_operands(q, k, b, b_ref, level)
        z = (sel * dec).astype(BF16)
        s_level = lax.dot_general(z, block_diagonal(z), NT_DIMS, preferred_element_type=F32)
        for r, rows in enumerate(pieces):
            if rows.start >> level & 1 or (1 << level) < SUBLANES:
                scores[r] = jnp.where(lvl_ref[rows, :] == level, s_level[rows], scores[r])
    p = jnp.concatenate(scores, axis=0).astype(BF16)
    v16 = v.astype(BF16)
    o = jnp.dot(p, block_diagonal(v16), preferred_element_type=F32)
    if not local_blocks:
        qk = q * k
        diag = [jnp.broadcast_to(jnp.sum(qk[:, hd], axis=-1, keepdims=True), (c, HG_D))
                for hd in heads]
        o = o + jnp.concatenate(diag, axis=1) * v
    st = st_ref[...]
    o = o + lax.dot_general((q * jnp.exp2(b)).astype(BF16), st.astype(BF16), NT_DIMS,
                            preferred_element_type=F32)
    b_last = b_ref[c - 1:c, :]
    k_dec = (k * jnp.exp2(b_last - b)).astype(BF16)
    update = lax.dot_general(v16, k_dec, TN_DIMS, preferred_element_type=F32)
    st_dec = jnp.exp2(b_last)
    for hd in heads:
        st_ref[hd, hd] = st[hd, hd] * st_dec[:, hd] + update[hd, hd]
    return o


def _project(u, w_ref, first, count=1):
    parts = [jnp.dot(u, w_ref[first + t], preferred_element_type=F32) for t in range(count)]
    return parts[0] if count == 1 else jnp.concatenate(parts, axis=1)


def _hgrn_project(p, slot, u_ref, w_ref, lbt_ref, q_ref, k_ref, v_ref, g_ref, b_ref):
    u = u_ref[...]
    tab = jnp.concatenate([lbt_ref[2 * p], lbt_ref[2 * p + 1]], axis=1)
    log_lb, log1m_lb, one_m_lb = tab[0:1], tab[1:2], tab[2:3]
    q_ref[slot] = _silu(_project(u, w_ref, BLK_HG_Q + p))
    zf = _project(u, w_ref, BLK_HG_F + p)
    e = jnp.exp(-jnp.abs(zf))
    one_p_e = 1.0 + e
    a2 = log1m_lb + jnp.minimum(zf, 0.0) - jnp.log(one_p_e)
    lf = jnp.maximum(log_lb, a2) + jnp.log(1.0 + jnp.exp(-jnp.abs(log_lb - a2)))
    lf2 = lf * LOG2_E
    k_ref[slot] = one_m_lb * (jnp.where(zf >= 0.0, e, 1.0) / one_p_e)
    v_ref[slot] = _project(u, w_ref, BLK_HG_V + p)
    g_ref[slot] = _silu(_project(u, w_ref, BLK_HG_G + p))
    drop = None
    for ck in range(lf2.shape[0] // HG_CHUNK):
        b_ref[slot, ck] = _cumsum_rows(lf2[ck * HG_CHUNK:(ck + 1) * HG_CHUNK])
        d = _hgrn_decay_drop(b_ref.at[slot, ck])
        drop = d if drop is None else jnp.maximum(drop, d)
    return drop


def _hgrn_compute(local_blocks, p, slot, lvl_ref, q_ref, k_ref, v_ref, g_ref, b_ref, oa_ref,
                  st_ref):
    tm = q_ref.shape[1]
    for ck in range(tm // HG_CHUNK):
        rows = slice(ck * HG_CHUNK, (ck + 1) * HG_CHUNK)
        o = _hgrn_chunk(local_blocks, q_ref[slot, rows, :], k_ref[slot, rows, :],
                        v_ref[slot, rows, :], st_ref.at[p], lvl_ref, b_ref.at[slot, ck])
        for j in range(2):
            cols = slice(j * HG_D, (j + 1) * HG_D)
            oj = o[:, cols]
            oj = oj * lax.rsqrt(jnp.mean(oj * oj, axis=-1, keepdims=True) + HEAD_NORM_EPS)
            oa_ref[p, rows, cols] = (oj * g_ref[slot, rows, cols]).astype(BF16)


def _retention_project(h, slot, u_ref, w_ref, cs_ref, qdec_ref, kdec_ref,
                       q_ref, qd_ref, k_ref, kd_ref, v_ref, g_ref):
    half = RET_DK // 2
    u = u_ref[...]
    cos, sin = cs_ref[0, :, 0:half], cs_ref[0, :, half:RET_DK]

    def rotated(block, dec_ref, plain_ref, scaled_ref):
        t = _project(u, w_ref, block)
        t1, t2 = t[:, 0:half], t[:, half:RET_DK]
        r1, r2 = t1 * cos - t2 * sin, t1 * sin + t2 * cos
        dec = dec_ref[h]
        plain_ref[slot] = jnp.concatenate([r1, r2], axis=1).astype(BF16)
        scaled_ref[slot] = jnp.concatenate([r1 * dec, r2 * dec], axis=1).astype(BF16)

    rotated(BLK_RET_Q + h, qdec_ref, q_ref, qd_ref)
    rotated(BLK_RET_K + h, kdec_ref, k_ref, kd_ref)
    wide = RET_DV // W_BLOCK
    v_ref[slot] = _project(u, w_ref, BLK_RET_V + wide * h, wide).astype(BF16)
    g_ref[slot] = _silu(_project(u, w_ref, BLK_RET_G + wide * h, wide))


def _retention_compute(h, slot, dmask_ref, cdec_ref, q_ref, qd_ref, k_ref, kd_ref, v_ref, g_ref,
                       ob_ref, st_ref):
    v16 = v_ref[slot]
    scores = lax.dot_general(q_ref[slot], k_ref[slot], NT_DIMS,
                             preferred_element_type=F32) * dmask_ref[h]
    st = st_ref[h]
    o = (jnp.dot(scores.astype(BF16), v16, preferred_element_type=F32)
         + jnp.dot(qd_ref[slot], st.astype(BF16), preferred_element_type=F32))
    st_ref[h] = cdec_ref[h][0:1, :] * st + lax.dot_general(kd_ref[slot], v16, TN_DIMS,
                                                           preferred_element_type=F32)
    o = o - jnp.mean(o, axis=-1, keepdims=True)
    o = o * lax.rsqrt(jnp.mean(o * o, axis=-1, keepdims=True) + HEAD_NORM_EPS)
    ob_ref[h] = (o * g_ref[slot]).astype(BF16)


def _mixer_kernel(alpha, x_ref, mod_ref, cs_ref, lbt_ref, lvl_ref, dmask_ref, qdec_ref, kdec_ref,
                  cdec_ref, w_ref, wpa_ref, wpb_ref, wo_ref, lng_ref, lnb_ref,
                  o_ref, u_ref, hq_ref, hk_ref, hv_ref, hg_ref,
                  rq_ref, rqd_ref, rk_ref, rkd_ref, rv_ref, rg_ref,
                  oa_ref, ob_ref, sth_ref, str_ref, b_ref):
    @pl.when(pl.program_id(1) == 0)
    def _():
        sth_ref[...] = jnp.zeros_like(sth_ref)
        str_ref[...] = jnp.zeros_like(str_ref)

    x = x_ref[0]
    shift, scale, gate = mod_ref[0, 0:1, :], mod_ref[0, 1:2, :], mod_ref[0, 2:3, :]
    u_ref[...] = (x * (1.0 + scale) + shift).astype(BF16)

    assert HG_PAIRS == RET_HEADS
    hgrn_act = (hq_ref, hk_ref, hv_ref, hg_ref, b_ref)
    ret_act = (rq_ref, rqd_ref, rk_ref, rkd_ref, rv_ref, rg_ref)
    drop = None
    for i in range(HG_PAIRS):
        d = _hgrn_project(i, i, u_ref, w_ref, lbt_ref, *hgrn_act)
        drop = d if drop is None else jnp.maximum(drop, d)

    def rest(local_blocks):
        for i in range(HG_PAIRS):
            _hgrn_compute(local_blocks, i, i, lvl_ref, *hgrn_act, oa_ref, sth_ref)
            _retention_project(i, i % 2, u_ref, w_ref, cs_ref, qdec_ref, kdec_ref, *ret_act)
            _retention_compute(i, i % 2, dmask_ref, cdec_ref, *ret_act, ob_ref, str_ref)
        merged = []
        n_half = D_MODEL // 2
        blocks_half = n_half // W_BLOCK
        for h in range(2):
            cols = slice(h * n_half, (h + 1) * n_half)
            ya = jnp.dot(oa_ref[0], wpa_ref[0, :, cols], preferred_element_type=F32)
            yb = jnp.dot(ob_ref[0], wpb_ref[0, :, cols], preferred_element_type=F32)
            for i in range(1, HG_PAIRS):
                ya = ya + jnp.dot(oa_ref[i], wpa_ref[i, :, cols], preferred_element_type=F32)
                yb = yb + jnp.dot(ob_ref[i], wpb_ref[i, :, cols], preferred_element_type=F32)
            ga = _project(u_ref[...], w_ref, BLK_GATES + h * blocks_half, blocks_half)
            gb = _project(u_ref[...], w_ref, BLK_GATES + (2 + h) * blocks_half, blocks_half)
            merged.append((jax.nn.sigmoid(ga) * ya + jax.nn.sigmoid(gb) * yb).astype(BF16))
        y = jnp.dot(jnp.concatenate(merged, axis=1), wo_ref[...], preferred_element_type=F32)
        o_ref[0] = _layer_norm(alpha * x_ref[0] + (1.0 + gate) * y, lng_ref[...], lnb_ref[...])

    lax.cond(drop <= LOCAL_MAX_DROP, functools.partial(rest, True), functools.partial(rest, False))


def _mixer_tables(tm):
    i = np.arange(HG_CHUNK)
    differ = i[:, None] ^ i[None, :]
    level = np.where(i[:, None] > i[None, :], np.floor(np.log2(np.maximum(differ, 1))), LEVEL_NONE)
    level[i, i] = LEVEL_DIAGONAL
    log_gamma = np.log1p(-np.exp2(-5.0 - np.arange(RET_HEADS)))[:, None, None]
    r = np.arange(tm, dtype=np.float64)
    diff = r[:, None] - r[None, :]
    scale = RET_DK ** -0.5
    dmask = np.where(diff >= 0, np.exp(log_gamma * np.maximum(diff, 0.0)), 0.0) * scale
    ones = np.ones((1, 1, RET_DK // 2))
    qdec = np.exp(log_gamma * (r[None, :, None] + 1.0)) * ones
    kdec = np.exp(log_gamma * (tm - 1.0 - r[None, :, None])) * scale * ones
    cdec = np.exp(log_gamma * tm) * np.ones((1, SUBLANES, RET_DV))
    level = np.tile(level, (1, 2))
    return (jnp.asarray(level, jnp.int32), jnp.asarray(dmask, F32), jnp.asarray(qdec, F32),
            jnp.asarray(kdec, F32), jnp.asarray(cdec, F32))


def _mixer(alpha, layer, x, mod, cos_sin, lbt, w_blocks, wpa, wpb, wo, ln_g, ln_b):
    batch, seq, d = x.shape
    tm = min(MIX_TM, seq)
    tables = _mixer_tables(tm)
    tile = pl.BlockSpec((1, tm, d), lambda b, s: (b, s, 0))
    consts = (lbt,) + tables
    weights = (wpa, wpb, wo, ln_g, ln_b)
    layer_blocks = pl.BlockSpec((None,) + w_blocks.shape[1:], lambda b, s: (layer, 0, 0, 0),
                                pipeline_mode=pl.Buffered(1))
    return pl.pallas_call(
        functools.partial(_mixer_kernel, alpha),
        grid=(batch, seq // tm),
        in_specs=[tile,
                  pl.BlockSpec((1, 6, d), lambda b, s: (b, 0, 0)),
                  pl.BlockSpec((1, tm, RET_DK), lambda b, s: (b, s, 0))]
                 + [_resident(a.shape) for a in consts] + [layer_blocks]
                 + [_resident(a.shape) for a in weights],
        out_specs=tile,
        out_shape=jax.ShapeDtypeStruct(x.shape, F32),
        scratch_shapes=[pltpu.VMEM((tm, d), BF16)]
                       + [pltpu.VMEM((HG_PAIRS, tm, 2 * HG_D), F32)] * 4
                       + [pltpu.VMEM((2, tm, RET_DK), BF16)] * 4
                       + [pltpu.VMEM((2, tm, RET_DV), BF16),
                          pltpu.VMEM((2, tm, RET_DV), F32),
                          pltpu.VMEM((HG_PAIRS, tm, 2 * HG_D), BF16),
                          pltpu.VMEM((RET_HEADS, tm, RET_DV), BF16),
                          pltpu.VMEM((HG_PAIRS, 2 * HG_D, 2 * HG_D), F32),
                          pltpu.VMEM((RET_HEADS, RET_DK, RET_DV), F32),
                          pltpu.VMEM((HG_PAIRS, tm // HG_CHUNK, HG_CHUNK, 2 * HG_D), F32)],
        compiler_params=pltpu.CompilerParams(
            dimension_semantics=("arbitrary", "arbitrary"), vmem_limit_bytes=VMEM_LIMIT),
        name="mixer_sublayer",
    )(x, mod, cos_sin, *consts, w_blocks, *weights)


def _mlp_kernel(alpha, x_ref, mod_ref, wup_ref, bup_ref, wdn_ref, bdn_ref, lng_ref, lnb_ref,
                o_ref, h_ref):
    shift, scale, gate = mod_ref[0, 3:4, :], mod_ref[0, 4:5, :], mod_ref[0, 5:6, :]
    for t in range(x_ref.shape[1] // MLP_SUB):
        rows = slice(t * MLP_SUB, (t + 1) * MLP_SUB)
        x = x_ref[0, rows, :]
        u = (x * (1.0 + scale) + shift).astype(BF16)
        for j in range(D_FF // D_MODEL):
            cols = slice(j * D_MODEL, (j + 1) * D_MODEL)
            hj = jnp.dot(u, wup_ref[:, cols], preferred_element_type=F32) + bup_ref[:, cols]
            h_ref[t, :, cols] = jnp.square(jnp.maximum(hj, 0.0)).astype(BF16)
        y = jnp.dot(h_ref[t], wdn_ref[...], preferred_element_type=F32) + bdn_ref[...]
        o_ref[0, rows, :] = _layer_norm(alpha * x + (1.0 + gate) * y, lng_ref[...], lnb_ref[...])


def _mlp(alpha, x, mod, wup, bup, wdn, bdn, ln_g, ln_b):
    batch, seq, d = x.shape
    tm = min(MLP_TM, seq)
    assert tm % MLP_SUB == 0
    tile = pl.BlockSpec((1, tm, d), lambda b, s: (b, s, 0))
    return pl.pallas_call(
        functools.partial(_mlp_kernel, alpha),
        grid=(batch, seq // tm),
        in_specs=[tile,
                  pl.BlockSpec((1, 6, d), lambda b, s: (b, 0, 0)),
                  _resident(wup.shape), _resident(bup.shape), _resident(wdn.shape),
                  _resident(bdn.shape), _resident(ln_g.shape), _resident(ln_b.shape)],
        out_specs=tile,
        out_shape=jax.ShapeDtypeStruct(x.shape, F32),
        scratch_shapes=[pltpu.VMEM((tm // MLP_SUB, MLP_SUB, D_FF), BF16)],
        compiler_params=pltpu.CompilerParams(
            dimension_semantics=("arbitrary", "arbitrary"), vmem_limit_bytes=VMEM_LIMIT),
        name="mlp_sublayer",
    )(x, mod, wup, bup, wdn, bdn, ln_g, ln_b)


def kernel(x, c, positions, lb_logits, w_ada, b_ada, w_in, w_pa, w_pb, w_o, ln1_g, ln1_b,
           w_up, b_up, w_down, b_down, ln2_g, ln2_b):
    depth = w_in.shape[0]
    batch, seq, d = x.shape
    alpha = (2 * depth) ** 0.25
    mod = _modulation(c, w_ada, b_ada).reshape(depth, batch, 6, d)
    lbt = _lower_bound_tables(lb_logits)
    lbt = lbt.reshape(depth, 8, HG_HEADS, HG_D).transpose(0, 2, 1, 3)
    theta = (ROPE_BASE ** (-jnp.linspace(0.0, 1.0, RET_DK // 2, dtype=F32))).reshape(1, -1)
    cos_sin = _rope_table(positions, theta)
    w_blocks = _regroup_in_projection(w_in)
    for l in range(depth):
        x = _mixer(alpha, l, x, mod[l], cos_sin, lbt[l], w_blocks,
                   w_pa[l].astype(BF16).reshape(HG_PAIRS, 2 * HG_D, d),
                   w_pb[l].astype(BF16).reshape(RET_HEADS, RET_DV, d),
                   w_o[l].astype(BF16), ln1_g[l].reshape(1, d), ln1_b[l].reshape(1, d))
        x = _mlp(alpha, x, mod[l], w_up[l].astype(BF16), b_up[l].reshape(1, -1),
                 w_down[l].astype(BF16), b_down[l].reshape(1, d),
                 ln2_g[l].reshape(1, d), ln2_b[l].reshape(1, d))
    return x
```
